```python
import math
import jax, jax.numpy as jnp
from jax import lax
import numpy as np

D_MODEL = 1024
BATCH = 2
SEQ = 8192
DEPTH = 2
DEC_BATCH = 32
DEC_SEQ = 4
PAST_LEN = 16384
PAGE_SIZE = 128

N_META = 16
N_A = DEPTH // 2
N_B = DEPTH - N_A
D_LRU = D_MODEL
N_LRU_BLOCKS = 8
LRU_BLOCK = D_LRU // N_LRU_BLOCKS
CONV_W = 4
LRU_C = 8.0
N_HEADS = 8
HEAD_DIM = D_MODEL // (2 * N_HEADS)
V_DIM = 2 * HEAD_DIM
ROPE_THETA = 10000.0
Q_BLOCK = 128
N_KEYS = 128
N_EXPERTS = N_KEYS * N_KEYS
PEER_HEADS = 8
PEER_TOPK = 16
D_KEY = 256
PEER_TOKEN_BLOCK = 256
EPS = 1e-6
NEG = -1e30

kernel_name = "yoco_hawk_diffattn_peer_step"


def rmsnorm(x, g):
    xf = x.astype(jnp.float32)
    y = xf * lax.rsqrt(jnp.mean(xf * xf, -1, keepdims=True) + EPS)
    return (y * g.astype(jnp.float32)).astype(x.dtype)


def rope(x, pos):
    half = HEAD_DIM // 2
    inv = ROPE_THETA ** (-jnp.arange(half, dtype=jnp.float32) / half)
    ang = pos.astype(jnp.float32)[:, None] * inv[None, :]
    cos = jnp.cos(ang)[None, :, None, None, :]
    sin = jnp.sin(ang)[None, :, None, None, :]
    xf = x.astype(jnp.float32)
    x1, x2 = xf[..., :half], xf[..., half:]
    return jnp.concatenate([x1 * cos - x2 * sin, x2 * cos + x1 * sin], -1).astype(x.dtype)


def causal_conv(u, buf, w, b):
    full = jnp.concatenate([buf.astype(u.dtype), u], 1)
    s = u.shape[1]
    y = sum(full[:, k:k + s] * w[k] for k in range(CONV_W)) + b
    return y, full[:, -(CONV_W - 1):]


def block_diag(x, w, b):
    xb = x.reshape(x.shape[:-1] + (N_LRU_BLOCKS, LRU_BLOCK))
    y = jnp.einsum('bsni,nij->bsnj', xb, w)
    return y.reshape(x.shape) + b


def rg_lru(x, h0, reset, w_a, b_a, w_x, b_x, lam):
    f32 = jnp.float32
    r = jax.nn.sigmoid(block_diag(x, w_a, b_a).astype(f32))
    i = jax.nn.sigmoid(block_diag(x, w_x, b_x).astype(f32))
    log_a = -LRU_C * r * jax.nn.softplus(-lam.astype(f32))
    a = jnp.exp(log_a)
    mult = jnp.where(reset[None, :, None], 1.0, jnp.sqrt(-jnp.expm1(2.0 * log_a)))
    u = mult * i * x.astype(f32)

    def step(h, inp):
        a_t, u_t = inp
        h = a_t * h + u_t
        return h, h

    h_t, hs = lax.scan(step, h0.astype(f32), (a.swapaxes(0, 1), u.swapaxes(0, 1)))
    return hs.swapaxes(0, 1).astype(x.dtype), h_t.astype(h0.dtype)


def recurrent_block(x, h0, buf0, reset, w_gate, w_in, conv_w, conv_b, w_a, b_a, w_x, b_x, lam, w_out):
    gate = jax.nn.gelu(x @ w_gate)
    u_c, new_buf = causal_conv(x @ w_in, buf0, conv_w, conv_b)
    hs, h_t = rg_lru(u_c, h0, reset, w_a, b_a, w_x, b_x, lam)
    return ((gate * hs) @ w_out).astype(x.dtype), h_t, new_buf


def peer(x, w_q, sub_keys, u_tab, v_tab):
    shp = x.shape
    xf = x.reshape(-1, D_MODEL)
    n = xf.shape[0]
    n_pad = -(-n // PEER_TOKEN_BLOCK) * PEER_TOKEN_BLOCK
    xf = jnp.pad(xf, ((0, n_pad - n), (0, 0)))

    def one_block(xb):
        q = (xb @ w_q).reshape(-1, PEER_HEADS, 2, D_KEY // 2)
        s = jnp.einsum('nhcd,hckd->nhck', q, sub_keys, preferred_element_type=jnp.float32)
        s_top, i_top = lax.top_k(s, PEER_TOPK)
        cand = s_top[:, :, 0, :, None] + s_top[:, :, 1, None, :]
        cand = cand.reshape(cand.shape[0], PEER_HEADS, PEER_TOPK * PEER_TOPK)
        g_s, j = lax.top_k(cand, PEER_TOPK)
        e = (jnp.take_along_axis(i_top[:, :, 0], j // PEER_TOPK, -1) * N_KEYS
             + jnp.take_along_axis(i_top[:, :, 1], j % PEER_TOPK, -1))
        gates = jax.nn.softmax(g_s, -1)
        u = u_tab[e]
        v = v_tab[e]
        act = jax.nn.gelu(jnp.einsum('nd,nhkd->nhk', xb, u, preferred_element_type=jnp.float32))
        return jnp.einsum('nhk,nhkd->nd', (gates * act).astype(xb.dtype), v).astype(xb.dtype)

    y = lax.map(one_block, xf.reshape(-1, PEER_TOKEN_BLOCK, D_MODEL))
    return y.reshape(n_pad, D_MODEL)[:n].reshape(shp)


def shared_kv(x, pos, g_kv, w_k, w_v, g_k):
    b, s = x.shape[:2]
    h = rmsnorm(x, g_kv)
    k = rope(rmsnorm((h @ w_k).reshape(b, s, N_HEADS, 2, HEAD_DIM), g_k), pos)
    v = (h @ w_v).reshape(b, s, N_HEADS, V_DIM)
    return k, v


def diff_queries(h, pos, w_q, g_q):
    b, s = h.shape[:2]
    return rope(rmsnorm((h @ w_q).reshape(b, s, N_HEADS, 2, HEAD_DIM), g_q), pos)


def prompt_attention(q, k, v):
    b, t = q.shape[:2]
    t_pad = -(-t // Q_BLOCK) * Q_BLOCK
    pad = ((0, 0), (0, t_pad - t), (0, 0), (0, 0), (0, 0))
    qp, kp, vp = jnp.pad(q, pad), jnp.pad(k, pad), jnp.pad(v, pad[:4])
    kpos = jnp.arange(t_pad)
    scale = HEAD_DIM ** -0.5

    def block(i):
        qb = lax.dynamic_slice_in_dim(qp, i * Q_BLOCK, Q_BLOCK, 1)
        s = jnp.einsum('bqhcd,bkhcd->bhcqk', qb, kp, preferred_element_type=jnp.float32) * scale
        qpos = i * Q_BLOCK + jnp.arange(Q_BLOCK)
        s = jnp.where(kpos[None, :] <= qpos[:, None], s, NEG)
        p = jax.nn.softmax(s, -1)
        return jnp.einsum('bhcqk,bkhe->bqhce', p.astype(vp.dtype), vp)

    out = lax.map(block, jnp.arange(t_pad // Q_BLOCK))
    out = jnp.moveaxis(out, 0, 1).reshape(b, t_pad, N_HEADS, 2, V_DIM)
    return out[:, :t]


def sample_attention(q, k_new, v_new, cache_k, cache_v, page_table):
    f32 = jnp.float32
    bd, s = q.shape[:2]
    scale = HEAD_DIM ** -0.5
    m0 = jnp.full((bd, N_HEADS, 2, s), NEG, f32)
    l0 = jnp.zeros((bd, N_HEADS, 2, s), f32)
    acc0 = jnp.zeros((bd, N_HEADS, 2, s, V_DIM), f32)

    def update(carry, sc, vb):
        m, l, acc = carry
        m_new = jnp.maximum(m, sc.max(-1))
        corr = jnp.exp(m - m_new)
        p = jnp.exp(sc - m_new[..., None])
        l = l * corr + p.sum(-1)
        acc = acc * corr[..., None] + jnp.einsum('bhcqk,bkhe->bhcqe', p, vb.astype(f32))
        return (m_new, l, acc)

    def page_step(carry, pages):
        kb = cache_k[pages]
        vb = cache_v[pages]
        sc = jnp.einsum('bqhcd,bkhcd->bhcqk', q, kb, preferred_element_type=f32) * scale
        return update(carry, sc, vb), None

    carry, _ = lax.scan(page_step, (m0, l0, acc0), page_table.T)
    sc = jnp.einsum('bqhcd,bkhcd->bhcqk', q, k_new, preferred_element_type=f32) * scale
    causal = jnp.arange(s)[None, :] <= jnp.arange(s)[:, None]
    m, l, acc = update(carry, jnp.where(causal, sc, NEG), v_new)
    out = acc / l[..., None]
    return out.transpose(0, 3, 1, 2, 4).astype(q.dtype)


def diff_combine(attn, lam_p, g_sub, w_o, lambda_init):
    b, s = attn.shape[:2]
    lp = lam_p.astype(jnp.float32)
    lam = jnp.exp(jnp.sum(lp[0] * lp[1])) - jnp.exp(jnp.sum(lp[2] * lp[3])) + lambda_init
    o = attn[:, :, :, 0] - lam.astype(attn.dtype) * attn[:, :, :, 1]
    o = rmsnorm(o, g_sub) * (1.0 - lambda_init)
    return o.reshape(b, s, N_HEADS * V_DIM) @ w_o


def setup_inputs(seed: int = 0) -> dict:
    key = jax.random.key(seed)
    ks = iter(jax.random.split(key, 48))
    f32 = jnp.float32

    def nrm(shape, scale):
        return jax.random.normal(next(ks), shape, f32) * scale

    n_pages = PAST_LEN // PAGE_SIZE
    n_used = DEC_BATCH * n_pages
    n_pool = n_used + max(1, n_used // 4)
    a_target = jax.random.uniform(next(ks), (N_A, D_LRU), f32, 0.9, 0.999)
    s_base = a_target ** (1.0 / LRU_C)
    lru_lambda = jnp.log(s_base / (1.0 - s_base))
    page_table = jax.random.permutation(next(ks), n_pool)[:n_used].reshape(DEC_BATCH, n_pages).astype(jnp.int32)
    return {
        "x_prompt": nrm((BATCH, SEQ, D_MODEL), 1.0),
        "x_sample": nrm((DEC_BATCH, DEC_SEQ, D_MODEL), 1.0),
        "state_lru_h": nrm((N_A, DEC_BATCH, D_LRU), 0.5),
        "state_conv": nrm((N_A, DEC_BATCH, CONV_W - 1, D_LRU), 1.0),
        "cache_k": nrm((n_pool, PAGE_SIZE, N_HEADS, 2, HEAD_DIM), 1.0),
        "cache_v": nrm((n_pool, PAGE_SIZE, N_HEADS, V_DIM), 1.0),
        "page_table": page_table,
        "meta_tokens": nrm((N_META, D_MODEL), 1.0),
        "g_mix": 1.0 + nrm((DEPTH, D_MODEL), 0.02),
        "g_ffn": 1.0 + nrm((DEPTH, D_MODEL), 0.02),
        "lru_w_gate": nrm((N_A, D_MODEL, D_LRU), D_MODEL ** -0.5),
        "lru_w_in": nrm((N_A, D_MODEL, D_LRU), D_MODEL ** -0.5),
        "lru_conv_w": nrm((N_A, CONV_W, D_LRU), CONV_W ** -0.5),
        "lru_conv_b": nrm((N_A, D_LRU), 0.02),
        "lru_w_a": nrm((N_A, N_LRU_BLOCKS, LRU_BLOCK, LRU_BLOCK), LRU_BLOCK ** -0.5),
        "lru_b_a": nrm((N_A, D_LRU), 0.02),
        "lru_w_x": nrm((N_A, N_LRU_BLOCKS, LRU_BLOCK, LRU_BLOCK), LRU_BLOCK ** -0.5),
        "lru_b_x": nrm((N_A, D_LRU), 0.02),
        "lru_lambda": lru_lambda,
        "lru_w_out": nrm((N_A, D_LRU, D_MODEL), D_LRU ** -0.5),
        "kv_norm": 1.0 + nrm((D_MODEL,), 0.02),
        "w_k": nrm((D_MODEL, N_HEADS * 2 * HEAD_DIM), D_MODEL ** -0.5),
        "w_v": nrm((D_MODEL, N_HEADS * V_DIM), D_MODEL ** -0.5),
        "k_norm": 1.0 + nrm((HEAD_DIM,), 0.02),
        "w_q": nrm((N_B, D_MODEL, N_HEADS * 2 * HEAD_DIM), D_MODEL ** -0.5),
        "q_norm": 1.0 + nrm((N_B, HEAD_DIM), 0.02),
        "lam_params": nrm((N_B, 4, HEAD_DIM), 0.1),
        "sub_norm": 1.0 + nrm((N_B, V_DIM), 0.02),
        "w_o": nrm((N_B, N_HEADS * V_DIM, D_MODEL), (N_HEADS * V_DIM) ** -0.5),
        "peer_w_q": nrm((DEPTH, D_MODEL, PEER_HEADS * D_KEY), D_MODEL ** -0.5),
        "peer_keys": nrm((DEPTH, PEER_HEADS, 2, N_KEYS, D_KEY // 2), (D_KEY // 2) ** -0.5),
        "peer_u": nrm((DEPTH, N_EXPERTS, D_MODEL), D_MODEL ** -0.5),
        "peer_v": nrm((DEPTH, N_EXPERTS, D_MODEL), PEER_HEADS ** -0.5),
    }


def reference(x_prompt, x_sample, state_lru_h, state_conv, cache_k, cache_v, page_table,
              meta_tokens, g_mix, g_ffn, lru_w_gate, lru_w_in, lru_conv_w, lru_conv_b,
              lru_w_a, lru_b_a, lru_w_x, lru_b_x, lru_lambda, lru_w_out,
              kv_norm, w_k, w_v, k_norm, w_q, q_norm, lam_params, sub_norm, w_o,
              peer_w_q, peer_keys, peer_u, peer_v):

    def run(x, pos, reset, h0, buf0, attend):
        new_h, new_buf = [], []
        k = v = None
        for layer in range(DEPTH):
            if layer < N_A:
                y, h_t, nb = recurrent_block(
                    rmsnorm(x, g_mix[layer]), h0[layer], buf0[layer], reset,
                    lru_w_gate[layer], lru_w_in[layer], lru_conv_w[layer], lru_conv_b[layer],
                    lru_w_a[layer], lru_b_a[layer], lru_w_x[layer], lru_b_x[layer],
                    lru_lambda[layer], lru_w_out[layer])
                new_h.append(h_t)
                new_buf.append(nb)
            else:
                if layer == N_A:
                    k, v = shared_kv(x, pos, kv_norm, w_k, w_v, k_norm)
                j = layer - N_A
                q = diff_queries(rmsnorm(x, g_mix[layer]), pos, w_q[j], q_norm[j])
                lambda_init = 0.8 - 0.6 * math.exp(-0.3 * layer)
                y = diff_combine(attend(q, k, v), lam_params[j], sub_norm[j], w_o[j], lambda_init)
            x = x + y.astype(x.dtype)
            x = x + peer(rmsnorm(x, g_ffn[layer]), peer_w_q[layer], peer_keys[layer],
                         peer_u[layer], peer_v[layer]).astype(x.dtype)
        return x, jnp.stack(new_h), jnp.stack(new_buf), k, v

    t = SEQ + N_META
    xp = jnp.concatenate([jnp.broadcast_to(meta_tokens.astype(x_prompt.dtype)[None], (x_prompt.shape[0], N_META, D_MODEL)), x_prompt], 1)
    pos_p = jnp.arange(t)
    h0_p = jnp.zeros((N_A, x_prompt.shape[0], D_LRU), x_prompt.dtype)
    buf0_p = jnp.zeros((N_A, x_prompt.shape[0], CONV_W - 1, D_LRU), x_prompt.dtype)
    out_p, h_p, buf_p, k_p, v_p = run(xp, pos_p, pos_p == 0, h0_p, buf0_p, prompt_attention)
    y_prompt = out_p[:, N_META:]

    pos_s = PAST_LEN + jnp.arange(x_sample.shape[1])
    reset_s = jnp.zeros((x_sample.shape[1],), bool)

    def attend_s(q, k, v):
        return sample_attention(q, k, v, cache_k, cache_v, page_table)

    y_sample, h_s, buf_s, k_s, v_s = run(x_sample, pos_s, reset_s, state_lru_h, state_conv, attend_s)

    return (y_prompt, y_sample, h_p, buf_p, k_p, v_p, h_s, buf_s, k_s, v_s)
```

```python
import functools
import math

import jax
import jax.numpy as jnp
from jax import lax
from jax.experimental import pallas as pl
from jax.experimental.pallas import tpu as pltpu

F32 = jnp.float32
BF16 = jnp.bfloat16

EPS = 1e-6
NEG = -1e30
LRU_C = 8.0
ROPE_THETA = 10000.0
CONV_W = 4
N_META = 16
PEER_TOPK = 16

LANE = 128
SUBLANE = 8
VMEM_LIMIT = 56 * 1024 * 1024

LRU_TILE = 640
ATT_TILE = 640
TOK_TILE = 512
ROUTE_TILE = 256
EXPERT_CHUNK = 1024
PAGES_PER_STEP = 4


def _cparams(sem):
    return pltpu.CompilerParams(dimension_semantics=sem, vmem_limit_bytes=VMEM_LIMIT)


def _rmsnorm(x, g):
    ms = jnp.mean(x * x, axis=-1, keepdims=True)
    return x * lax.rsqrt(ms + EPS) * g


def _full(shape):
    n = len(shape)
    return pl.BlockSpec(shape, lambda *_: (0,) * n)


def _lru_kernel(x_ref, h0_ref, buf0_ref, g_ref, wg_ref, wi_ref, cw_ref, cb_ref, wa_ref, ba_ref,
                wx_ref, bx_ref, lam_ref, wo_ref,
                xo_ref, ht_ref, buft_ref,
                ubuf, a_s, u_s, hs_s, h_s, *, nseq, rows, reset_first, last_t):
    ti = pl.program_id(1)
    d = x_ref.shape[-1]
    tail = (CONV_W - 1) * nseq
    off = -(-tail // SUBLANE) * SUBLANE

    @pl.when(ti == 0)
    def _():
        h_s[...] = h0_ref[0]
        ubuf[off - tail:off, :] = buf0_ref[0]

    x = x_ref[...]
    xb = _rmsnorm(x, g_ref[...]).astype(BF16)
    gate = jax.nn.gelu(jnp.dot(xb, wg_ref[...], preferred_element_type=F32))
    ubuf[off:off + rows, :] = jnp.dot(xb, wi_ref[...], preferred_element_type=F32)

    y = cb_ref[...]
    for k in range(CONV_W):
        s = off - (CONV_W - 1 - k) * nseq
        y = y + ubuf[s:s + rows, :] * cw_ref[k:k + 1, :]

    steps = rows // nseq
    lt_tile, lt_loc = divmod(last_t, steps)

    @pl.when(ti == lt_tile)
    def _():
        s = off + (lt_loc - (CONV_W - 2)) * nseq
        buft_ref[0] = ubuf[s:s + tail, :]

    ubuf[off - tail:off, :] = ubuf[off + rows - tail:off + rows, :]

    lam = -lam_ref[...]
    softplus = jnp.maximum(lam, 0.0) + jnp.log1p(jnp.exp(-jnp.abs(lam)))
    nblk = wa_ref.shape[0]
    bw = d // nblk
    if reset_first:
        row = lax.broadcasted_iota(jnp.int32, (rows, bw), 0)
        is_t0 = row < jnp.where(ti == 0, nseq, 0)
    for n in range(nblk):
        sl = slice(n * bw, (n + 1) * bw)
        yn = y[:, sl]
        ybn = yn.astype(BF16)
        r = jax.nn.sigmoid(jnp.dot(ybn, wa_ref[n], preferred_element_type=F32) + ba_ref[:, sl])
        i = jax.nn.sigmoid(jnp.dot(ybn, wx_ref[n], preferred_element_type=F32) + bx_ref[:, sl])
        log_a = -LRU_C * r * softplus[:, sl]
        a = jnp.exp(log_a)
        mult = jnp.sqrt(-jnp.tanh(log_a) * (a * a + 1.0))
        if reset_first:
            mult = jnp.where(is_t0, 1.0, mult)
        a_s[:, sl] = a
        u_s[:, sl] = mult * i * yn

    if nseq == 1:
        row8 = lax.broadcasted_iota(jnp.int32, (SUBLANE, d), 0)

        def group(gi, h):
            base = pl.multiple_of(gi * SUBLANE, SUBLANE)
            a8 = a_s[pl.ds(base, SUBLANE), :]
            u8 = u_s[pl.ds(base, SUBLANE), :]
            hs8 = jnp.zeros_like(a8)
            for r_ in range(SUBLANE):
                h = a8[r_:r_ + 1, :] * h + u8[r_:r_ + 1, :]
                hs8 = jnp.where(row8 == r_, h, hs8)
            hs_s[pl.ds(base, SUBLANE), :] = hs8
            return h
        h = lax.fori_loop(0, rows // SUBLANE, group, h_s[...])
    else:
        h = h_s[...]
        for t in range(steps):
            h = a_s[t * nseq:(t + 1) * nseq, :] * h + u_s[t * nseq:(t + 1) * nseq, :]
            hs_s[t * nseq:(t + 1) * nseq, :] = h
    h_s[...] = h

    @pl.when(ti == lt_tile)
    def _():
        ht_ref[0] = hs_s[lt_loc * nseq:(lt_loc + 1) * nseq, :]

    go = (gate * hs_s[...]).astype(BF16)
    xo_ref[...] = x + jnp.dot(go, wo_ref[...], preferred_element_type=F32)


def _lru_call(xall, h0, buf0, w, *, nbatch, nseq, rows, ntile, row0, reset_first, last_t):
    nt, d = xall.shape
    tail = (CONV_W - 1) * nseq
    off = -(-tail // SUBLANE) * SUBLANE
    blk0 = row0 // rows
    xmap = lambda b, t: (blk0 + b * ntile + t, 0)
    smap = lambda b, t: (b, 0, 0)
    kern = functools.partial(_lru_kernel, nseq=nseq, rows=rows, reset_first=reset_first,
                             last_t=last_t)
    return pl.pallas_call(
        kern,
        grid=(nbatch, ntile),
        in_specs=[pl.BlockSpec((rows, d), xmap),
                  pl.BlockSpec((1, nseq, d), smap),
                  pl.BlockSpec((1, tail, d), smap)] + [_full(a.shape) for a in w],
        out_specs=[pl.BlockSpec((rows, d), xmap),
                   pl.BlockSpec((1, nseq, d), smap),
                   pl.BlockSpec((1, tail, d), smap)],
        out_shape=[jax.ShapeDtypeStruct((nt, d), F32),
                   jax.ShapeDtypeStruct((nbatch, nseq, d), F32),
                   jax.ShapeDtypeStruct((nbatch, tail, d), F32)],
        scratch_shapes=[pltpu.VMEM((off + rows, d), F32),
                        pltpu.VMEM((rows, d), F32),
                        pltpu.VMEM((rows, d), F32),
                        pltpu.VMEM((rows, d), F32),
                        pltpu.VMEM((nseq, d), F32)],
        input_output_aliases={0: 0},
        compiler_params=_cparams(("arbitrary", "arbitrary")),
        name="rg_lru_block",
    )(xall, h0, buf0, *w)


def _top_values(v, n):
    tops = []
    for _ in range(n):
        m = jnp.max(v, axis=0, keepdims=True)
        tops.append(m)
        v = jnp.where(v == m, -jnp.inf, v)
    removed = jnp.sum(jnp.where(v == -jnp.inf, 1.0, 0.0), axis=0, keepdims=True)
    return tops, removed


def _route_kernel(x_ref, g_ref, wqt_ref, keys_ref,
                  xnt_ref, s1_ref, c1_ref, s2_ref, e2_ref, tau_ref, flag_ref, q_s):
    nheads = s1_ref.shape[0]
    nkeys = s1_ref.shape[1]
    xn = _rmsnorm(x_ref[...], g_ref[...])
    xnt = xn.T.astype(BF16)
    xnt_ref[...] = xnt
    q_s[...] = jnp.dot(wqt_ref[...], xnt, preferred_element_type=F32)

    row16 = lax.broadcasted_iota(jnp.int32, (PEER_TOPK, q_s.shape[1]), 0)

    def head(h, carry):
        tops, dup, expd, sc = [], [], [], []
        for c in range(2):
            r0 = pl.multiple_of((h * 2 + c) * nkeys, nkeys)
            qhc = q_s[pl.ds(r0, nkeys), :].astype(BF16)
            s = jnp.dot(keys_ref[h * 2 + c], qhc, preferred_element_type=F32)
            t, removed = _top_values(s, PEER_TOPK + 1)
            tops.append(t)
            dup.append(removed != float(PEER_TOPK + 1))
            expd.append(jnp.exp(s - t[0]))
            sc.append(s)
        b16 = jnp.zeros((PEER_TOPK, tops[1][0].shape[1]), F32)
        for t in range(PEER_TOPK):
            b16 = jnp.where(row16 == t, tops[1][t], b16)
        cand = jnp.concatenate([tops[0][p] + b16 for p in range(PEER_TOPK)], axis=0)
        best, removed = _top_values(cand, PEER_TOPK)
        tau = best[PEER_TOPK - 1]
        z = jnp.ones_like(tau)
        for t in range(1, PEER_TOPK):
            z = z + jnp.exp(best[t] - best[0])
        tie = dup[0] | dup[1] | (removed != float(PEER_TOPK))
        tie = tie | (tops[0][PEER_TOPK] + tops[1][0] >= tau) | (tops[0][0] + tops[1][PEER_TOPK] >= tau)
        s1_ref[h] = sc[0]
        c1_ref[h] = expd[0] / z
        s2_ref[h] = sc[1]
        e2_ref[h] = expd[1]
        tau_ref[h] = tau
        flag_ref[h] = jnp.where(tie, 1.0, 0.0)
        return carry

    lax.fori_loop(0, nheads, head, 0)


def _route_call(xall, g, wqt, keys):
    nt, d = xall.shape
    nhc, nkeys, _ = keys.shape
    nheads = nhc // 2
    tm = ROUTE_TILE
    tok = lambda i: (0, 0, i)
    big = jax.ShapeDtypeStruct((nheads, nkeys, nt), F32)
    small = jax.ShapeDtypeStruct((nheads, 1, nt), F32)
    return pl.pallas_call(
        _route_kernel,
        grid=(nt // tm,),
        in_specs=[pl.BlockSpec((tm, d), lambda i: (i, 0)), _full(g.shape), _full(wqt.shape),
                  _full(keys.shape)],
        out_specs=[pl.BlockSpec((d, tm), lambda i: (0, i))]
        + [pl.BlockSpec((nheads, nkeys, tm), tok)] * 4
        + [pl.BlockSpec((nheads, 1, tm), tok)] * 2,
        out_shape=[jax.ShapeDtypeStruct((d, nt), BF16), big, big, big, big, small, small],
        scratch_shapes=[pltpu.VMEM((wqt.shape[0], tm), F32)],
        compiler_params=_cparams(("arbitrary",)),
        name="peer_route",
    )(xall, g, wqt, keys)


def _expert_kernel(x_ref, xnt_ref, s1_ref, c1_ref, s2_ref, e2_ref, tau_ref, u_ref, vt_ref,
                   xo_ref, acc, a_s, w_s):
    c = pl.program_id(1)
    nheads, nkeys, tm = s2_ref.shape
    ec = u_ref.shape[0]

    @pl.when(c == 0)
    def _():
        acc[...] = jnp.zeros_like(acc)

    a_s[...] = jnp.dot(u_ref[...], xnt_ref[...], preferred_element_type=F32)
    i0 = pl.multiple_of(c * (ec // nkeys), SUBLANE)

    def lane_group(l, carry):
        l0 = pl.multiple_of(l * LANE, LANE)
        s1 = [s1_ref[h, pl.ds(i0, SUBLANE), pl.ds(l0, LANE)] for h in range(nheads)]
        c1 = [c1_ref[h, pl.ds(i0, SUBLANE), pl.ds(l0, LANE)] for h in range(nheads)]
        tau = [tau_ref[h, :, pl.ds(l0, LANE)] for h in range(nheads)]
        for ii in range(ec // nkeys):
            g = jnp.zeros((nkeys, LANE), F32)
            for h in range(nheads):
                s2 = s2_ref[h, :, pl.ds(l0, LANE)]
                e2 = e2_ref[h, :, pl.ds(l0, LANE)]
                sel = (s1[h][ii:ii + 1, :] + s2) >= tau[h]
                g = g + jnp.where(sel, e2 * c1[h][ii:ii + 1, :], 0.0)
            rows = slice(ii * nkeys, (ii + 1) * nkeys)
            act = jax.nn.gelu(a_s[rows, pl.ds(l0, LANE)])
            w_s[rows, pl.ds(l0, LANE)] = (g * act).astype(BF16)
        return carry

    lax.fori_loop(0, tm // LANE, lane_group, 0)
    acc[...] += jnp.dot(vt_ref[...], w_s[...], preferred_element_type=F32)

    @pl.when(c == pl.num_programs(1) - 1)
    def _():
        xo_ref[...] = x_ref[...] + acc[...].T


def _expert_call(xall, xnt, s1, c1, s2, e2, tau, u, vt):
    nt, d = xall.shape
    nheads, nkeys, _ = s1.shape
    nexp = u.shape[0]
    tm, ec = TOK_TILE, EXPERT_CHUNK
    tok = lambda i, c: (0, 0, i)
    return pl.pallas_call(
        _expert_kernel,
        grid=(nt // tm, nexp // ec),
        in_specs=[pl.BlockSpec((tm, d), lambda i, c: (i, 0)),
                  pl.BlockSpec((d, tm), lambda i, c: (0, i))]
        + [pl.BlockSpec((nheads, nkeys, tm), tok)] * 4
        + [pl.BlockSpec((nheads, 1, tm), tok),
           pl.BlockSpec((ec, d), lambda i, c: (c, 0)),
           pl.BlockSpec((d, ec), lambda i, c: (0, c))],
        out_specs=pl.BlockSpec((tm, d), lambda i, c: (i, 0)),
        out_shape=jax.ShapeDtypeStruct((nt, d), F32),
        scratch_shapes=[pltpu.VMEM((d, tm), F32), pltpu.VMEM((ec, tm), F32),
                        pltpu.VMEM((ec, tm), BF16)],
        input_output_aliases={0: 0},
        compiler_params=_cparams(("arbitrary", "arbitrary")),
        name="peer_experts",
    )(xall, xnt, s1, c1, s2, e2, tau, u, vt)


def _peer(xall, g, wqt, keys, u, vt):
    xnt, s1, c1, s2, e2, tau, _ = _route_call(xall, g, wqt, keys)
    return _expert_call(xall, xnt, s1, c1, s2, e2, tau, u, vt)


def _exact_group_sum(sq, ones_bd):
    hi = sq.astype(BF16)
    r1 = sq - hi.astype(F32)
    mid = r1.astype(BF16)
    lo = (r1 - mid.astype(F32)).astype(BF16)
    dot = lambda t: jnp.dot(t, ones_bd, preferred_element_type=F32)
    return dot(hi) + dot(mid) + dot(lo)


def _head_norm_rope(y, g2, cos, sin_signed, ones_bd, first_half, head_dim):
    out = []
    for n in range(y.shape[1] // LANE):
        blk = y[:, n * LANE:(n + 1) * LANE]
        ms = _exact_group_sum(blk * blk, ones_bd) / head_dim
        z = blk * lax.rsqrt(ms + EPS) * g2
        partner = jnp.where(first_half, pltpu.roll(z, LANE - head_dim // 2, 1),
                            pltpu.roll(z, head_dim // 2, 1))
        out.append(z * cos + partner * sin_signed)
    return jnp.concatenate(out, axis=1)


def _qkv_kernel(x_ref, cos_ref, sin_ref, gkv_ref, gq_ref, wk_ref, wv_ref, wq_ref, gk2_ref,
                gq2_ref, k_ref, v_ref, kb_ref, vb_ref, qb_ref, *, head_dim):
    x = x_ref[...]
    cos = cos_ref[...]
    sin_signed = sin_ref[...]
    lane = lax.broadcasted_iota(jnp.int32, (1, LANE), 1)
    first_half = (lane % head_dim) < head_dim // 2
    r = lax.broadcasted_iota(jnp.int32, (LANE, LANE), 0) // head_dim
    cidx = lax.broadcasted_iota(jnp.int32, (LANE, LANE), 1) // head_dim
    ones_bd = jnp.where(r == cidx, 1.0, 0.0).astype(BF16)

    hk = _rmsnorm(x, gkv_ref[...]).astype(BF16)
    k = _head_norm_rope(jnp.dot(hk, wk_ref[...], preferred_element_type=F32), gk2_ref[...],
                        cos, sin_signed, ones_bd, first_half, head_dim)
    v = jnp.dot(hk, wv_ref[...], preferred_element_type=F32)
    hq = _rmsnorm(x, gq_ref[...]).astype(BF16)
    q = _head_norm_rope(jnp.dot(hq, wq_ref[...], preferred_element_type=F32), gq2_ref[...],
                        cos, sin_signed, ones_bd, first_half, head_dim)
    k_ref[...] = k
    v_ref[...] = v
    kb_ref[...] = k.astype(BF16)
    vb_ref[...] = v.astype(BF16)
    qb_ref[...] = (q * head_dim ** -0.5).astype(BF16)


def _qkv_call(xall, cos, sin_signed, gkv, gq, wk, wv, wq, gk2, gq2, head_dim):
    nt, d = xall.shape
    tm = TOK_TILE
    row = lambda i: (i, 0)
    ws = [gkv, gq, wk, wv, wq, gk2, gq2]
    return pl.pallas_call(
        functools.partial(_qkv_kernel, head_dim=head_dim),
        grid=(nt // tm,),
        in_specs=[pl.BlockSpec((tm, d), row), pl.BlockSpec((tm, LANE), row),
                  pl.BlockSpec((tm, LANE), row)] + [_full(a.shape) for a in ws],
        out_specs=[pl.BlockSpec((tm, d), row)] * 5,
        out_shape=[jax.ShapeDtypeStruct((nt, d), F32)] * 2
        + [jax.ShapeDtypeStruct((nt, d), BF16)] * 3,
        compiler_params=_cparams(("arbitrary",)),
        name="qkv_proj",
    )(xall, cos, sin_signed, *ws)


def _prompt_attn_kernel(q_ref, k_ref, v_ref, z0_ref, z1_ref, a0_ref, a1_ref, *, head_dim):
    del z0_ref, z1_ref
    qi = pl.program_id(2)
    tq = q_ref.shape[0]
    tk = tq
    q = q_ref[...]
    lane = lax.broadcasted_iota(jnp.int32, (1, LANE), 1)
    zero = jnp.zeros_like(q)
    qs = jnp.concatenate([jnp.where(lane < head_dim, q, zero),
                          jnp.where(lane >= head_dim, q, zero)], axis=0)

    def update(carry, j, masked):
        m, l, acc = carry
        k0 = pl.multiple_of(j * tk, tk)
        kj = k_ref[pl.ds(k0, tk), :]
        vj = v_ref[pl.ds(k0, tk), :]
        s = lax.dot_general(qs, kj, (((1,), (1,)), ((), ())), preferred_element_type=F32)
        if masked:
            qpos = lax.broadcasted_iota(jnp.int32, (2 * tq, tk), 0) % tq
            kpos = lax.broadcasted_iota(jnp.int32, (2 * tq, tk), 1)
            s = jnp.where(kpos <= qpos, s, NEG)
        m_new = jnp.maximum(m, jnp.max(s, axis=-1, keepdims=True))
        corr = jnp.exp(m - m_new)
        p = jnp.exp(s - m_new)
        l = l * corr + jnp.sum(p, axis=-1, keepdims=True)
        acc = acc * corr + jnp.dot(p.astype(BF16), vj, preferred_element_type=F32)
        return m_new, l, acc

    init = (jnp.full((2 * tq, 1), NEG, F32), jnp.zeros((2 * tq, 1), F32),
            jnp.zeros((2 * tq, LANE), F32))
    carry = lax.fori_loop(0, qi, lambda j, cr: update(cr, j, False), init)
    m, l, acc = update(carry, qi, True)
    out = acc / l
    a0_ref[...] = out[:tq]
    a1_ref[...] = out[tq:]


def _prompt_attn_call(qb, kb, vb, a0, a1, nbatch, tp, head_dim):
    nt, d = qb.shape
    nheads = d // LANE
    tq = ATT_TILE
    nq = tp // tq
    qmap = lambda b, h, i: (b * nq + i, h)
    kvmap = lambda b, h, i: (b, h)
    return pl.pallas_call(
        functools.partial(_prompt_attn_kernel, head_dim=head_dim),
        grid=(nbatch, nheads, nq),
        in_specs=[pl.BlockSpec((tq, LANE), qmap), pl.BlockSpec((tp, LANE), kvmap),
                  pl.BlockSpec((tp, LANE), kvmap),
                  pl.BlockSpec(memory_space=pl.ANY), pl.BlockSpec(memory_space=pl.ANY)],
        out_specs=[pl.BlockSpec((tq, LANE), qmap)] * 2,
        out_shape=[jax.ShapeDtypeStruct((nt, d), F32)] * 2,
        input_output_aliases={3: 0, 4: 1},
        compiler_params=_cparams(("arbitrary", "arbitrary", "arbitrary")),
        name="prompt_attention",
    )(qb, kb, vb, a0, a1)


def _decode_attn_kernel(pt_ref, q_ref, kn_ref, vn_ref, *refs, head_dim, npg):
    del pt_ref
    kp = refs[:npg]
    vp = refs[npg:2 * npg]
    a0_ref, a1_ref, qblk, m_s, l_s, acc = refs[2 * npg:]
    p = pl.program_id(1)
    nrow, d = qblk.shape
    nheads = d // LANE
    ntok = nrow // (2 * nheads)
    nt_dot = lambda a, b: lax.dot_general(a, b, (((1,), (1,)), ((), ())),
                                          preferred_element_type=F32)

    @pl.when(p == 0)
    def _():
        q = q_ref[0].astype(F32)
        row = lax.broadcasted_iota(jnp.int32, (nrow, d), 0)
        col = lax.broadcasted_iota(jnp.int32, (nrow, d), 1)
        rep = jnp.concatenate(
            [jnp.broadcast_to(q[t:t + 1, :], (nheads, d)) for t in range(ntok)] * 2, axis=0)
        mine = (col // LANE == row % nheads) & ((col % LANE) // head_dim == row // (ntok * nheads))
        qblk[...] = jnp.where(mine, rep, 0.0).astype(BF16)
        m_s[...] = jnp.full_like(m_s, NEG)
        l_s[...] = jnp.zeros_like(l_s)
        acc[...] = jnp.zeros_like(acc)

    def update(s, vs):
        m = m_s[...]
        m_new = jnp.maximum(m, jnp.max(s, axis=-1, keepdims=True))
        corr = jnp.exp(m - m_new)
        pr = jnp.exp(s - m_new)
        l_s[...] = l_s[...] * corr + jnp.sum(pr, axis=-1, keepdims=True)
        m_s[...] = m_new
        pv = acc[...] * corr
        w = pr.shape[1] // len(vs)
        for i, v in enumerate(vs):
            pv = pv + jnp.dot(pr[:, i * w:(i + 1) * w].astype(BF16), v,
                              preferred_element_type=F32)
        acc[...] = pv

    qb = qblk[...]
    s = jnp.concatenate([nt_dot(qb, kp[i][0].astype(BF16)) for i in range(npg)], axis=1)
    update(s, [vp[i][0].astype(BF16) for i in range(npg)])

    @pl.when(p == pl.num_programs(1) - 1)
    def _():
        kn = kn_ref[0]
        s = nt_dot(qblk[...], kn.astype(BF16))
        trow = (lax.broadcasted_iota(jnp.int32, s.shape, 0) // nheads) % ntok
        tcol = lax.broadcasted_iota(jnp.int32, s.shape, 1)
        update(jnp.where(tcol <= trow, s, NEG), [vn_ref[0].astype(BF16)])
        out = acc[...] / l_s[...]
        row = lax.broadcasted_iota(jnp.int32, (nheads, d), 0)
        col = lax.broadcasted_iota(jnp.int32, (nheads, d), 1)
        for ci, dst in enumerate((a0_ref, a1_ref)):
            for t in range(ntok):
                r0 = (ci * ntok + t) * nheads
                blk = jnp.where(col // LANE == row, out[r0:r0 + nheads, :], 0.0)
                dst[0, t:t + 1, :] = jnp.sum(blk, axis=0, keepdims=True)


def _decode_attn_call(page_table, q, kn, vn, cache_k, cache_v, head_dim):
    nb, ntok_pad, d = kn.shape
    ntok = q.shape[1]
    npages = page_table.shape[1]
    npg = PAGES_PER_STEP
    psz = cache_k.shape[1]
    nheads = d // LANE
    bmap = lambda b, p, pt: (b, 0, 0)
    pmaps = [functools.partial(lambda b, p, pt, i: (pt[b, p * npg + i], 0, 0), i=i)
             for i in range(npg)]
    grid_spec = pltpu.PrefetchScalarGridSpec(
        num_scalar_prefetch=1,
        grid=(nb, npages // npg),
        in_specs=[pl.BlockSpec((1, ntok, d), bmap), pl.BlockSpec((1, ntok_pad, d), bmap),
                  pl.BlockSpec((1, ntok_pad, d), bmap)]
        + [pl.BlockSpec((1, psz, d), pm) for pm in pmaps] * 2,
        out_specs=[pl.BlockSpec((1, ntok, d), bmap)] * 2,
        scratch_shapes=[pltpu.VMEM((2 * ntok * nheads, d), BF16),
                        pltpu.VMEM((2 * ntok * nheads, 1), F32),
                        pltpu.VMEM((2 * ntok * nheads, 1), F32),
                        pltpu.VMEM((2 * ntok * nheads, d), F32)])
    return pl.pallas_call(
        functools.partial(_decode_attn_kernel, head_dim=head_dim, npg=npg),
        grid_spec=grid_spec,
        out_shape=[jax.ShapeDtypeStruct((nb, ntok, d), F32)] * 2,
        compiler_params=_cparams(("arbitrary", "arbitrary")),
        name="decode_attention",
    )(page_table, q, kn, vn, *([cache_k] * npg), *([cache_v] * npg))


def _combine_kernel(x_ref, a0_ref, a1_ref, lp_ref, gs_ref, wo_ref, xo_ref, *, lambda_init):
    lp = lp_ref[...]
    lam = (jnp.exp(jnp.sum(lp[0:1] * lp[1:2], axis=-1, keepdims=True))
           - jnp.exp(jnp.sum(lp[2:3] * lp[3:4], axis=-1, keepdims=True)) + lambda_init)
    o = a0_ref[...] - lam * a1_ref[...]
    parts = []
    for h in range(o.shape[1] // LANE):
        parts.append(_rmsnorm(o[:, h * LANE:(h + 1) * LANE], gs_ref[...]) * (1.0 - lambda_init))
    ob = jnp.concatenate(parts, axis=1).astype(BF16)
    xo_ref[...] = x_ref[...] + jnp.dot(ob, wo_ref[...], preferred_element_type=F32)


def _combine_call(xall, a0, a1, lp, gs, wo, lambda_init):
    nt, d = xall.shape
    tm = TOK_TILE
    row = lambda i: (i, 0)
    return pl.pallas_call(
        functools.partial(_combine_kernel, lambda_init=lambda_init),
        grid=(nt // tm,),
        in_specs=[pl.BlockSpec((tm, d), row)] * 3 + [_full(lp.shape), _full(gs.shape),
                                                      _full(wo.shape)],
        out_specs=pl.BlockSpec((tm, d), row),
        out_shape=jax.ShapeDtypeStruct((nt, d), F32),
        input_output_aliases={0: 0},
        compiler_params=_cparams(("arbitrary",)),
        name="diff_combine_out_proj",
    )(xall, a0, a1, lp, gs, wo)


def _round_up(n, m):
    return -(-n // m) * m


def kernel(x_prompt, x_sample, state_lru_h, state_conv, cache_k, cache_v, page_table, meta_tokens, g_mix, g_ffn, lru_w_gate, lru_w_in, lru_conv_w, lru_conv_b, lru_w_a, lru_b_a, lru_w_x, lru_b_x, lru_lambda, lru_w_out, kv_norm, w_k, w_v, k_norm, w_q, q_norm, lam_params, sub_norm, w_o, peer_w_q, peer_keys, peer_u, peer_v):
    nbatch, seq, d = x_prompt.shape
    nb_dec, dec_seq, _ = x_sample.shape
    depth = g_mix.shape[0]
    assert depth == 2 and lru_w_gate.shape[0] == 1 and w_q.shape[0] == 1
    head_dim = k_norm.shape[0]
    nheads = d // (2 * head_dim)
    assert 2 * head_dim == LANE
    past_len = page_table.shape[1] * cache_k.shape[1]
    t_real = seq + N_META
    tp = _round_up(t_real, math.lcm(LRU_TILE, ATT_TILE))
    ns = nb_dec * dec_seq
    row_s = nbatch * tp
    assert row_s % ns == 0
    nt = _round_up(row_s + ns, math.lcm(TOK_TILE, ROUTE_TILE))
    row2 = lambda a: a.reshape(1, -1)

    meta = jnp.broadcast_to(meta_tokens.astype(F32)[None], (nbatch, N_META, d))
    xp = jnp.concatenate([meta, x_prompt, jnp.zeros((nbatch, tp - t_real, d), F32)], axis=1)
    xs = x_sample.transpose(1, 0, 2).reshape(ns, d)
    xall = jnp.concatenate([xp.reshape(row_s, d), xs, jnp.zeros((nt - row_s - ns, d), F32)], 0)

    lw = [row2(g_mix[0]), lru_w_gate[0].astype(BF16), lru_w_in[0].astype(BF16), lru_conv_w[0],
          row2(lru_conv_b[0]), lru_w_a[0].astype(BF16), row2(lru_b_a[0]),
          lru_w_x[0].astype(BF16), row2(lru_b_x[0]), row2(lru_lambda[0]),
          lru_w_out[0].astype(BF16)]
    xall, h_p, buf_p = _lru_call(
        xall, jnp.zeros((nbatch, 1, d), F32), jnp.zeros((nbatch, CONV_W - 1, d), F32), lw,
        nbatch=nbatch, nseq=1, rows=LRU_TILE, ntile=tp // LRU_TILE, row0=0, reset_first=True,
        last_t=t_real - 1)
    buf0_s = state_conv[0].transpose(1, 0, 2).reshape(1, (CONV_W - 1) * nb_dec, d)
    xall, h_s, buf_s = _lru_call(
        xall, state_lru_h[0][None], buf0_s, lw,
        nbatch=1, nseq=nb_dec, rows=ns, ntile=1, row0=row_s, reset_first=False,
        last_t=dec_seq - 1)

    def peer_layer(x, layer):
        keys = peer_keys[layer].reshape(-1, peer_keys.shape[3], peer_keys.shape[4]).astype(BF16)
        return _peer(x, row2(g_ffn[layer]), peer_w_q[layer].T.astype(BF16), keys,
                     peer_u[layer].astype(BF16), peer_v[layer].T.astype(BF16))

    xall = peer_layer(xall, 0)

    half = head_dim // 2
    inv = ROPE_THETA ** (-jnp.arange(half, dtype=F32) / half)
    rows = jnp.arange(nt)
    pos = jnp.where(rows < row_s, rows % tp,
                    jnp.where(rows < row_s + ns, past_len + (rows - row_s) // nb_dec, 0))
    ang = pos.astype(F32)[:, None] * inv[None, :]
    cos = jnp.tile(jnp.cos(ang), (1, LANE // half))
    sin = jnp.sin(ang)
    sin_signed = jnp.tile(jnp.concatenate([-sin, sin], axis=1), (1, LANE // head_dim))
    tile2 = lambda g: jnp.tile(g, LANE // head_dim).reshape(1, LANE)
    k_all, v_all, kb, vb, qb = _qkv_call(
        xall, cos, sin_signed, row2(kv_norm), row2(g_mix[1]), w_k.astype(BF16),
        w_v.astype(BF16), w_q[0].astype(BF16), tile2(k_norm), tile2(q_norm[0]), head_dim)

    zeros = jnp.zeros((nt, d), F32)
    a0, a1 = _prompt_attn_call(qb, kb, vb, zeros, zeros, nbatch, tp, head_dim)

    to_seq = lambda a: a[row_s:row_s + ns].reshape(dec_seq, nb_dec, d).transpose(1, 0, 2)
    pad_tok = ((0, 0), (0, LANE - dec_seq), (0, 0))
    a0s, a1s = _decode_attn_call(
        page_table, to_seq(qb), jnp.pad(to_seq(k_all), pad_tok), jnp.pad(to_seq(v_all), pad_tok),
        cache_k.reshape(cache_k.shape[0], cache_k.shape[1], d),
        cache_v.reshape(cache_v.shape[0], cache_v.shape[1], d), head_dim)
    from_seq = lambda a: a.transpose(1, 0, 2).reshape(ns, d)
    a0 = lax.dynamic_update_slice(a0, from_seq(a0s), (row_s, 0))
    a1 = lax.dynamic_update_slice(a1, from_seq(a1s), (row_s, 0))

    lambda_init = 0.8 - 0.6 * math.exp(-0.3 * 1)
    xall = _combine_call(xall, a0, a1, lam_params[0], row2(sub_norm[0]), w_o[0].astype(BF16),
                         lambda_init)
    xall = peer_layer(xall, 1)

    prompt = lambda a: a[:row_s].reshape(nbatch, tp, d)
    y_prompt = prompt(xall)[:, N_META:t_real]
    y_sample = to_seq(xall)
    k_p = prompt(k_all)[:, :t_real].reshape(nbatch, t_real, nheads, 2, head_dim)
    v_p = prompt(v_all)[:, :t_real].reshape(nbatch, t_real, nheads, 2 * head_dim)
    k_s = to_seq(k_all).reshape(nb_dec, dec_seq, nheads, 2, head_dim)
    v_s = to_seq(v_all).reshape(nb_dec, dec_seq, nheads, 2 * head_dim)
    conv_s = buf_s.reshape(CONV_W - 1, nb_dec, d).transpose(1, 0, 2)[None]
    return (y_prompt, y_sample, h_p.reshape(1, nbatch, d), buf_p[None], k_p, v_p,
            h_s, conv_s, k_s, v_s)
```

```python
import functools
import math

import jax
import jax.numpy as jnp
from jax import lax
from jax.experimental import pallas as pl
from jax.experimental.pallas import tpu as pltpu

F32 = jnp.float32
BF16 = jnp.bfloat16

EPS = 1e-6
NEG = -1e30
LRU_C = 8.0
ROPE_THETA = 10000.0
CONV_W = 4
N_META = 16
PEER_TOPK = 16

LANE = 128
SUBLANE = 8
VMEM_LIMIT = 56 * 1024 * 1024

LRU_TILE = 640
ATT_TILE = 640
TOK_TILE = 512
ROUTE_TILE = 256
EXPERT_CHUNK = 1024
GATE_BLOCK = 32
PAGES_PER_STEP = 4


def _cparams(sem):
    return pltpu.CompilerParams(dimension_semantics=sem, vmem_limit_bytes=VMEM_LIMIT)


def _rmsnorm(x, g):
    ms = jnp.mean(x * x, axis=-1, keepdims=True)
    return x * lax.rsqrt(ms + EPS) * g


def _full(shape):
    n = len(shape)
    return pl.BlockSpec(shape, lambda *_: (0,) * n)


def _lru_kernel(x_ref, h0_ref, buf0_ref, g_ref, wg_ref, wi_ref, cw_ref, cb_ref, wa_ref, ba_ref,
                wx_ref, bx_ref, lam_ref, wo_ref,
                xo_ref, ht_ref, buft_ref,
                ubuf, a_s, u_s, hs_s, h_s, *, nseq, rows, reset_first, last_t):
    ti = pl.program_id(1)
    d = x_ref.shape[-1]
    tail = (CONV_W - 1) * nseq
    off = -(-tail // SUBLANE) * SUBLANE

    @pl.when(ti == 0)
    def _():
        h_s[...] = h0_ref[0]
        ubuf[off - tail:off, :] = buf0_ref[0]

    x = x_ref[...]
    xb = _rmsnorm(x, g_ref[...]).astype(BF16)
    gate = jax.nn.gelu(jnp.dot(xb, wg_ref[...], preferred_element_type=F32))
    ubuf[off:off + rows, :] = jnp.dot(xb, wi_ref[...], preferred_element_type=F32)

    y = cb_ref[...]
    for k in range(CONV_W):
        s = off - (CONV_W - 1 - k) * nseq
        y = y + ubuf[s:s + rows, :] * cw_ref[k:k + 1, :]

    steps = rows // nseq
    lt_tile, lt_loc = divmod(last_t, steps)

    @pl.when(ti == lt_tile)
    def _():
        s = off + (lt_loc - (CONV_W - 2)) * nseq
        buft_ref[0] = ubuf[s:s + tail, :]

    ubuf[off - tail:off, :] = ubuf[off + rows - tail:off + rows, :]

    lam = -lam_ref[...]
    softplus = jnp.maximum(lam, 0.0) + jnp.log1p(jnp.exp(-jnp.abs(lam)))
    nblk = wa_ref.shape[0]
    bw = d // nblk
    if reset_first:
        row = lax.broadcasted_iota(jnp.int32, (rows, bw), 0)
        is_t0 = row < jnp.where(ti == 0, nseq, 0)
    for n in range(nblk):
        sl = slice(n * bw, (n + 1) * bw)
        yn = y[:, sl]
        ybn = yn.astype(BF16)
        r = jax.nn.sigmoid(jnp.dot(ybn, wa_ref[n], preferred_element_type=F32) + ba_ref[:, sl])
        i = jax.nn.sigmoid(jnp.dot(ybn, wx_ref[n], preferred_element_type=F32) + bx_ref[:, sl])
        log_a = -LRU_C * r * softplus[:, sl]
        a = jnp.exp(log_a)
        mult = jnp.sqrt(-jnp.tanh(log_a) * (a * a + 1.0))
        if reset_first:
            mult = jnp.where(is_t0, 1.0, mult)
        a_s[:, sl] = a
        u_s[:, sl] = mult * i * yn

    if nseq == 1:
        row8 = lax.broadcasted_iota(jnp.int32, (SUBLANE, d), 0)

        def group(gi, h):
            base = pl.multiple_of(gi * SUBLANE, SUBLANE)
            a8 = a_s[pl.ds(base, SUBLANE), :]
            u8 = u_s[pl.ds(base, SUBLANE), :]
            hs8 = jnp.zeros_like(a8)
            for r_ in range(SUBLANE):
                h = a8[r_:r_ + 1, :] * h + u8[r_:r_ + 1, :]
                hs8 = jnp.where(row8 == r_, h, hs8)
            hs_s[pl.ds(base, SUBLANE), :] = hs8
            return h
        h = lax.fori_loop(0, rows // SUBLANE, group, h_s[...])
    else:
        h = h_s[...]
        for t in range(steps):
            h = a_s[t * nseq:(t + 1) * nseq, :] * h + u_s[t * nseq:(t + 1) * nseq, :]
            hs_s[t * nseq:(t + 1) * nseq, :] = h
    h_s[...] = h

    @pl.when(ti == lt_tile)
    def _():
        ht_ref[0] = hs_s[lt_loc * nseq:(lt_loc + 1) * nseq, :]

    go = (gate * hs_s[...]).astype(BF16)
    xo_ref[...] = x + jnp.dot(go, wo_ref[...], preferred_element_type=F32)


def _lru_call(xall, h0, buf0, w, *, nbatch, nseq, rows, ntile, row0, reset_first, last_t):
    nt, d = xall.shape
    tail = (CONV_W - 1) * nseq
    off = -(-tail // SUBLANE) * SUBLANE
    blk0 = row0 // rows
    xmap = lambda b, t: (blk0 + b * ntile + t, 0)
    smap = lambda b, t: (b, 0, 0)
    kern = functools.partial(_lru_kernel, nseq=nseq, rows=rows, reset_first=reset_first,
                             last_t=last_t)
    return pl.pallas_call(
        kern,
        grid=(nbatch, ntile),
        in_specs=[pl.BlockSpec((rows, d), xmap),
                  pl.BlockSpec((1, nseq, d), smap),
                  pl.BlockSpec((1, tail, d), smap)] + [_full(a.shape) for a in w],
        out_specs=[pl.BlockSpec((rows, d), xmap),
                   pl.BlockSpec((1, nseq, d), smap),
                   pl.BlockSpec((1, tail, d), smap)],
        out_shape=[jax.ShapeDtypeStruct((nt, d), F32),
                   jax.ShapeDtypeStruct((nbatch, nseq, d), F32),
                   jax.ShapeDtypeStruct((nbatch, tail, d), F32)],
        scratch_shapes=[pltpu.VMEM((off + rows, d), F32),
                        pltpu.VMEM((rows, d), F32),
                        pltpu.VMEM((rows, d), F32),
                        pltpu.VMEM((rows, d), F32),
                        pltpu.VMEM((nseq, d), F32)],
        input_output_aliases={0: 0},
        compiler_params=_cparams(("arbitrary", "arbitrary")),
        name="rg_lru_block",
    )(xall, h0, buf0, *w)


def _top_values(v, n):
    tops = []
    for _ in range(n):
        m = jnp.max(v, axis=0, keepdims=True)
        tops.append(m)
        v = jnp.where(v == m, -jnp.inf, v)
    removed = jnp.sum(jnp.where(v == -jnp.inf, 1.0, 0.0), axis=0, keepdims=True)
    return tops, removed


def _route_kernel(x_ref, g_ref, wqt_ref, keys_ref,
                  xnt_ref, s1_ref, c1_ref, s2_ref, e2_ref, tau_ref, flag_ref, q_s):
    nheads = s1_ref.shape[0]
    nkeys = s1_ref.shape[1]
    xn = _rmsnorm(x_ref[...], g_ref[...])
    xnt = xn.T.astype(BF16)
    xnt_ref[...] = xnt
    q_s[...] = jnp.dot(wqt_ref[...], xnt, preferred_element_type=F32)

    row16 = lax.broadcasted_iota(jnp.int32, (PEER_TOPK, q_s.shape[1]), 0)

    def head(h, carry):
        tops, dup, expd, sc = [], [], [], []
        for c in range(2):
            r0 = pl.multiple_of((h * 2 + c) * nkeys, nkeys)
            qhc = q_s[pl.ds(r0, nkeys), :].astype(BF16)
            s = jnp.dot(keys_ref[h * 2 + c], qhc, preferred_element_type=F32)
            t, removed = _top_values(s, PEER_TOPK + 1)
            tops.append(t)
            dup.append(removed != float(PEER_TOPK + 1))
            expd.append(jnp.exp(s - t[0]))
            sc.append(s)
        b16 = jnp.zeros((PEER_TOPK, tops[1][0].shape[1]), F32)
        for t in range(PEER_TOPK):
            b16 = jnp.where(row16 == t, tops[1][t], b16)
        cand = jnp.concatenate([tops[0][p] + b16 for p in range(PEER_TOPK)], axis=0)
        best, removed = _top_values(cand, PEER_TOPK)
        tau = best[PEER_TOPK - 1]
        z = jnp.ones_like(tau)
        for t in range(1, PEER_TOPK):
            z = z + jnp.exp(best[t] - best[0])
        tie = dup[0] | dup[1] | (removed != float(PEER_TOPK))
        tie = tie | (tops[0][PEER_TOPK] + tops[1][0] >= tau) | (tops[0][0] + tops[1][PEER_TOPK] >= tau)
        s1_ref[h] = sc[0]
        c1_ref[h] = expd[0] / z
        s2_ref[h] = sc[1]
        e2_ref[h] = expd[1]
        tau_ref[h] = tau
        flag_ref[h] = jnp.where(tie, 1.0, 0.0)
        return carry

    lax.fori_loop(0, nheads, head, 0)


def _route_call(xall, g, wqt, keys):
    nt, d = xall.shape
    nhc, nkeys, _ = keys.shape
    nheads = nhc // 2
    tm = ROUTE_TILE
    tok = lambda i: (0, 0, i)
    big = jax.ShapeDtypeStruct((nheads, nkeys, nt), F32)
    small = jax.ShapeDtypeStruct((nheads, 1, nt), F32)
    return pl.pallas_call(
        _route_kernel,
        grid=(nt // tm,),
        in_specs=[pl.BlockSpec((tm, d), lambda i: (i, 0)), _full(g.shape), _full(wqt.shape),
                  _full(keys.shape)],
        out_specs=[pl.BlockSpec((d, tm), lambda i: (0, i))]
        + [pl.BlockSpec((nheads, nkeys, tm), tok)] * 4
        + [pl.BlockSpec((nheads, 1, tm), tok)] * 2,
        out_shape=[jax.ShapeDtypeStruct((d, nt), BF16), big, big, big, big, small, small],
        scratch_shapes=[pltpu.VMEM((wqt.shape[0], tm), F32)],
        compiler_params=_cparams(("arbitrary",)),
        name="peer_route",
    )(xall, g, wqt, keys)


def _expert_kernel(x_ref, xnt_ref, s1_ref, c1_ref, s2_ref, e2_ref, tau_ref, u_ref, vt_ref,
                   xo_ref, acc, a_s, w_s, *, jblk):
    c = pl.program_id(1)
    nheads, nkeys, tm = s2_ref.shape
    ec = u_ref.shape[0]
    nrow_i = ec // nkeys
    njb = nkeys // jblk
    slab = nrow_i * jblk
    mm_lanes = 2 * LANE
    i0 = pl.multiple_of(c * nrow_i, SUBLANE)

    @pl.when(c == 0)
    def _():
        acc[...] = jnp.zeros_like(acc)

    def gates(l0, jb):
        lanes = pl.ds(l0, LANE)
        jrows = slice(jb * jblk, (jb + 1) * jblk)
        g = [jnp.zeros((jblk, LANE), F32) for _ in range(nrow_i)]
        for h in range(nheads):
            s1 = s1_ref[h, pl.ds(i0, SUBLANE), lanes]
            c1 = c1_ref[h, pl.ds(i0, SUBLANE), lanes]
            s2 = s2_ref[h, jrows, lanes]
            e2 = e2_ref[h, jrows, lanes]
            tau = tau_ref[h, :, lanes]
            for ii in range(nrow_i):
                sel = (s1[ii:ii + 1, :] + s2) >= tau
                g[ii] = g[ii] + jnp.where(sel, e2 * c1[ii:ii + 1, :], 0.0)
        for ii in range(nrow_i):
            rows = slice(jb * slab + ii * jblk, jb * slab + (ii + 1) * jblk)
            w_s[rows, lanes] = (g[ii] * jax.nn.gelu(a_s[rows, lanes])).astype(BF16)

    def piece(l, carry):
        m0 = pl.multiple_of(l * mm_lanes, mm_lanes)
        lanes = pl.ds(m0, mm_lanes)

        for jb in range(njb):
            rows = slice(jb * slab, (jb + 1) * slab)
            a_s[rows, lanes] = jnp.dot(u_ref[rows, :], xnt_ref[:, lanes],
                                       preferred_element_type=F32)
            for sub in range(mm_lanes // LANE):
                gates(pl.multiple_of(m0 + sub * LANE, LANE), jb)
            acc[:, lanes] += jnp.dot(vt_ref[:, rows], w_s[rows, lanes],
                                     preferred_element_type=F32)
        return carry

    lax.fori_loop(0, tm // mm_lanes, piece, 0)

    @pl.when(c == pl.num_programs(1) - 1)
    def _():
        xo_ref[...] = x_ref[...] + acc[...].T


def _expert_call(xall, xnt, s1, c1, s2, e2, tau, u, vt):
    nt, d = xall.shape
    nheads, nkeys, _ = s1.shape
    nexp = u.shape[0]
    tm, ec = TOK_TILE, EXPERT_CHUNK
    tok = lambda i, c: (0, 0, i)
    return pl.pallas_call(
        functools.partial(_expert_kernel, jblk=GATE_BLOCK),
        grid=(nt // tm, nexp // ec),
        in_specs=[pl.BlockSpec((tm, d), lambda i, c: (i, 0)),
                  pl.BlockSpec((d, tm), lambda i, c: (0, i))]
        + [pl.BlockSpec((nheads, nkeys, tm), tok)] * 4
        + [pl.BlockSpec((nheads, 1, tm), tok),
           pl.BlockSpec((ec, d), lambda i, c: (c, 0)),
           pl.BlockSpec((d, ec), lambda i, c: (0, c))],
        out_specs=pl.BlockSpec((tm, d), lambda i, c: (i, 0)),
        out_shape=jax.ShapeDtypeStruct((nt, d), F32),
        scratch_shapes=[pltpu.VMEM((d, tm), F32), pltpu.VMEM((ec, tm), F32),
                        pltpu.VMEM((ec, tm), BF16)],
        input_output_aliases={0: 0},
        compiler_params=_cparams(("arbitrary", "arbitrary")),
        name="peer_experts",
    )(xall, xnt, s1, c1, s2, e2, tau, u, vt)


def _slab_order(table, nkeys):
    nexp, d = table.shape
    nrow_i = EXPERT_CHUNK // nkeys
    t = table.reshape(nexp // EXPERT_CHUNK, nrow_i, nkeys // GATE_BLOCK, GATE_BLOCK, d)
    return t.transpose(0, 2, 1, 3, 4).reshape(nexp, d)


def _peer(xall, g, wqt, keys, u, vt):
    xnt, s1, c1, s2, e2, tau, _ = _route_call(xall, g, wqt, keys)
    return _expert_call(xall, xnt, s1, c1, s2, e2, tau, u, vt)


def _exact_group_sum(sq, ones_bd):
    hi = sq.astype(BF16)
    r1 = sq - hi.astype(F32)
    mid = r1.astype(BF16)
    lo = (r1 - mid.astype(F32)).astype(BF16)
    dot = lambda t: jnp.dot(t, ones_bd, preferred_element_type=F32)
    return dot(hi) + dot(mid) + dot(lo)


def _head_norm_rope(y, g2, cos, sin_signed, ones_bd, first_half, head_dim):
    out = []
    for n in range(y.shape[1] // LANE):
        blk = y[:, n * LANE:(n + 1) * LANE]
        ms = _exact_group_sum(blk * blk, ones_bd) / head_dim
        z = blk * lax.rsqrt(ms + EPS) * g2
        partner = jnp.where(first_half, pltpu.roll(z, LANE - head_dim // 2, 1),
                            pltpu.roll(z, head_dim // 2, 1))
        out.append(z * cos + partner * sin_signed)
    return jnp.concatenate(out, axis=1)


def _qkv_kernel(x_ref, cos_ref, sin_ref, gkv_ref, gq_ref, wk_ref, wv_ref, wq_ref, gk2_ref,
                gq2_ref, k_ref, v_ref, kb_ref, vb_ref, qb_ref, *, head_dim):
    x = x_ref[...]
    cos = cos_ref[...]
    sin_signed = sin_ref[...]
    lane = lax.broadcasted_iota(jnp.int32, (1, LANE), 1)
    first_half = (lane % head_dim) < head_dim // 2
    r = lax.broadcasted_iota(jnp.int32, (LANE, LANE), 0) // head_dim
    cidx = lax.broadcasted_iota(jnp.int32, (LANE, LANE), 1) // head_dim
    ones_bd = jnp.where(r == cidx, 1.0, 0.0).astype(BF16)

    hk = _rmsnorm(x, gkv_ref[...]).astype(BF16)
    k = _head_norm_rope(jnp.dot(hk, wk_ref[...], preferred_element_type=F32), gk2_ref[...],
                        cos, sin_signed, ones_bd, first_half, head_dim)
    v = jnp.dot(hk, wv_ref[...], preferred_element_type=F32)
    hq = _rmsnorm(x, gq_ref[...]).astype(BF16)
    q = _head_norm_rope(jnp.dot(hq, wq_ref[...], preferred_element_type=F32), gq2_ref[...],
                        cos, sin_signed, ones_bd, first_half, head_dim)
    k_ref[...] = k
    v_ref[...] = v
    kb_ref[...] = k.astype(BF16)
    vb_ref[...] = v.astype(BF16)
    qb_ref[...] = (q * head_dim ** -0.5).astype(BF16)


def _qkv_call(xall, cos, sin_signed, gkv, gq, wk, wv, wq, gk2, gq2, head_dim):
    nt, d = xall.shape
    tm = TOK_TILE
    row = lambda i: (i, 0)
    ws = [gkv, gq, wk, wv, wq, gk2, gq2]
    return pl.pallas_call(
        functools.partial(_qkv_kernel, head_dim=head_dim),
        grid=(nt // tm,),
        in_specs=[pl.BlockSpec((tm, d), row), pl.BlockSpec((tm, LANE), row),
                  pl.BlockSpec((tm, LANE), row)] + [_full(a.shape) for a in ws],
        out_specs=[pl.BlockSpec((tm, d), row)] * 5,
        out_shape=[jax.ShapeDtypeStruct((nt, d), F32)] * 2
        + [jax.ShapeDtypeStruct((nt, d), BF16)] * 3,
        compiler_params=_cparams(("arbitrary",)),
        name="qkv_proj",
    )(xall, cos, sin_signed, *ws)


def _prompt_attn_kernel(q_ref, k_ref, v_ref, z0_ref, z1_ref, a0_ref, a1_ref, *, head_dim):
    del z0_ref, z1_ref
    qi = pl.program_id(2)
    tq = q_ref.shape[0]
    tk = tq
    q = q_ref[...]
    lane = lax.broadcasted_iota(jnp.int32, (1, LANE), 1)
    zero = jnp.zeros_like(q)
    qs = jnp.concatenate([jnp.where(lane < head_dim, q, zero),
                          jnp.where(lane >= head_dim, q, zero)], axis=0)

    def update(carry, j, masked):
        m, l, acc = carry
        k0 = pl.multiple_of(j * tk, tk)
        kj = k_ref[pl.ds(k0, tk), :]
        vj = v_ref[pl.ds(k0, tk), :]
        s = lax.dot_general(qs, kj, (((1,), (1,)), ((), ())), preferred_element_type=F32)
        if masked:
            qpos = lax.broadcasted_iota(jnp.int32, (2 * tq, tk), 0) % tq
            kpos = lax.broadcasted_iota(jnp.int32, (2 * tq, tk), 1)
            s = jnp.where(kpos <= qpos, s, NEG)
        m_new = jnp.maximum(m, jnp.max(s, axis=-1, keepdims=True))
        corr = jnp.exp(m - m_new)
        p = jnp.exp(s - m_new)
        l = l * corr + jnp.sum(p, axis=-1, keepdims=True)
        acc = acc * corr + jnp.dot(p.astype(BF16), vj, preferred_element_type=F32)
        return m_new, l, acc

    init = (jnp.full((2 * tq, 1), NEG, F32), jnp.zeros((2 * tq, 1), F32),
            jnp.zeros((2 * tq, LANE), F32))
    carry = lax.fori_loop(0, qi, lambda j, cr: update(cr, j, False), init)
    m, l, acc = update(carry, qi, True)
    out = acc / l
    a0_ref[...] = out[:tq]
    a1_ref[...] = out[tq:]


def _prompt_attn_call(qb, kb, vb, a0, a1, nbatch, tp, head_dim):
    nt, d = qb.shape
    nheads = d // LANE
    tq = ATT_TILE
    nq = tp // tq
    qmap = lambda b, h, i: (b * nq + i, h)
    kvmap = lambda b, h, i: (b, h)
    return pl.pallas_call(
        functools.partial(_prompt_attn_kernel, head_dim=head_dim),
        grid=(nbatch, nheads, nq),
        in_specs=[pl.BlockSpec((tq, LANE), qmap), pl.BlockSpec((tp, LANE), kvmap),
                  pl.BlockSpec((tp, LANE), kvmap),
                  pl.BlockSpec(memory_space=pl.ANY), pl.BlockSpec(memory_space=pl.ANY)],
        out_specs=[pl.BlockSpec((tq, LANE), qmap)] * 2,
        out_shape=[jax.ShapeDtypeStruct((nt, d), F32)] * 2,
        input_output_aliases={3: 0, 4: 1},
        compiler_params=_cparams(("arbitrary", "arbitrary", "arbitrary")),
        name="prompt_attention",
    )(qb, kb, vb, a0, a1)


def _decode_attn_kernel(pt_ref, q_ref, kn_ref, vn_ref, *refs, head_dim, npg):
    del pt_ref
    kp = refs[:npg]
    vp = refs[npg:2 * npg]
    o_ref, qblk, expand, hmask, m_s, l_s, acc = refs[2 * npg:]
    p = pl.program_id(1)
    nrow, d = qblk.shape
    nheads = d // LANE
    ntok = nrow // (2 * nheads)
    psz = expand.shape[0]

    @pl.when(p == 0)
    def _():
        q = q_ref[0].astype(F32)
        row = lax.broadcasted_iota(jnp.int32, (nrow, d), 0)
        col = lax.broadcasted_iota(jnp.int32, (nrow, d), 1)
        rep = jnp.concatenate(
            [jnp.broadcast_to(q[t:t + 1, :], (nheads, d)) for t in range(ntok)] * 2, axis=0)
        mine = (col // LANE == row % nheads) & ((col % LANE) // head_dim == row // (ntok * nheads))
        qblk[...] = jnp.where(mine, rep, 0.0).astype(BF16)
        pos = lax.broadcasted_iota(jnp.int32, expand.shape, 0)
        ecol = lax.broadcasted_iota(jnp.int32, expand.shape, 1)
        expand[...] = jnp.where(ecol // nheads == pos, 1.0, 0.0).astype(BF16)
        hrow = lax.broadcasted_iota(jnp.int32, hmask.shape, 0)
        hcol = lax.broadcasted_iota(jnp.int32, hmask.shape, 1)
        hmask[...] = jnp.where(hcol % nheads == hrow % nheads, 1.0, 0.0)
        m_s[...] = jnp.full_like(m_s, NEG)
        l_s[...] = jnp.zeros_like(l_s)
        acc[...] = jnp.zeros_like(acc)

    def update(s, vs):
        m = m_s[...]
        m_new = jnp.maximum(m, jnp.max(s, axis=-1, keepdims=True))
        corr = jnp.exp(m - m_new)
        pr = jnp.exp(s - m_new)
        l_s[...] = l_s[...] * corr + jnp.sum(pr, axis=-1, keepdims=True)
        m_s[...] = m_new
        pv = acc[...] * corr
        for i, v in enumerate(vs):
            spread = jnp.dot(pr[:, i * psz:(i + 1) * psz].astype(BF16), expand[...],
                             preferred_element_type=F32) * hmask[...]
            pv = pv + jnp.dot(spread.astype(BF16), v, preferred_element_type=F32)
        acc[...] = pv

    qb = qblk[...]
    s = jnp.concatenate([jnp.dot(qb, kp[i][0].astype(BF16), preferred_element_type=F32)
                         for i in range(npg)], axis=1)
    update(s, [vp[i][0].astype(BF16) for i in range(npg)])

    @pl.when(p == pl.num_programs(1) - 1)
    def _():
        s = jnp.dot(qblk[...], kn_ref[0].astype(BF16), preferred_element_type=F32)
        trow = (lax.broadcasted_iota(jnp.int32, s.shape, 0) // nheads) % ntok
        tcol = lax.broadcasted_iota(jnp.int32, s.shape, 1)
        update(jnp.where(tcol <= trow, s, NEG), [vn_ref[0].astype(BF16)])
        o_ref[0] = acc[...] / l_s[...]


def _decode_attn_call(page_table, q, knt, vn, cache_kt, cache_v, head_dim):
    nb, ntok, d = q.shape
    npages = page_table.shape[1]
    npg = PAGES_PER_STEP
    psz = cache_kt.shape[2]
    nheads = d // LANE
    nrow = 2 * ntok * nheads
    bmap = lambda b, p, pt: (b, 0, 0)
    pmaps = [functools.partial(lambda b, p, pt, i: (pt[b, p * npg + i], 0, 0), i=i)
             for i in range(npg)]
    grid_spec = pltpu.PrefetchScalarGridSpec(
        num_scalar_prefetch=1,
        grid=(nb, npages // npg),
        in_specs=[pl.BlockSpec((1, ntok, d), bmap), pl.BlockSpec((1, d, psz), bmap),
                  pl.BlockSpec((1, psz * nheads, LANE), bmap)]
        + [pl.BlockSpec((1, d, psz), pm) for pm in pmaps]
        + [pl.BlockSpec((1, psz * nheads, LANE), pm) for pm in pmaps],
        out_specs=pl.BlockSpec((1, nrow, LANE), bmap),
        scratch_shapes=[pltpu.VMEM((nrow, d), BF16),
                        pltpu.VMEM((psz, psz * nheads), BF16),
                        pltpu.VMEM((nrow, psz * nheads), F32),
                        pltpu.VMEM((nrow, 1), F32),
                        pltpu.VMEM((nrow, 1), F32),
                        pltpu.VMEM((nrow, LANE), F32)])
    return pl.pallas_call(
        functools.partial(_decode_attn_kernel, head_dim=head_dim, npg=npg),
        grid_spec=grid_spec,
        out_shape=jax.ShapeDtypeStruct((nb, nrow, LANE), F32),
        compiler_params=_cparams(("arbitrary", "arbitrary")),
        name="decode_attention",
    )(page_table, q, knt, vn, *([cache_kt] * npg), *([cache_v] * npg))


def _combine_kernel(x_ref, a0_ref, a1_ref, lp_ref, gs_ref, wo_ref, xo_ref, *, lambda_init):
    lp = lp_ref[...]
    lam = (jnp.exp(jnp.sum(lp[0:1] * lp[1:2], axis=-1, keepdims=True))
           - jnp.exp(jnp.sum(lp[2:3] * lp[3:4], axis=-1, keepdims=True)) + lambda_init)
    o = a0_ref[...] - lam * a1_ref[...]
    parts = []
    for h in range(o.shape[1] // LANE):
        parts.append(_rmsnorm(o[:, h * LANE:(h + 1) * LANE], gs_ref[...]) * (1.0 - lambda_init))
    ob = jnp.concatenate(parts, axis=1).astype(BF16)
    xo_ref[...] = x_ref[...] + jnp.dot(ob, wo_ref[...], preferred_element_type=F32)


def _combine_call(xall, a0, a1, lp, gs, wo, lambda_init):
    nt, d = xall.shape
    tm = TOK_TILE
    row = lambda i: (i, 0)
    return pl.pallas_call(
        functools.partial(_combine_kernel, lambda_init=lambda_init),
        grid=(nt // tm,),
        in_specs=[pl.BlockSpec((tm, d), row)] * 3 + [_full(lp.shape), _full(gs.shape),
                                                      _full(wo.shape)],
        out_specs=pl.BlockSpec((tm, d), row),
        out_shape=jax.ShapeDtypeStruct((nt, d), F32),
        input_output_aliases={0: 0},
        compiler_params=_cparams(("arbitrary",)),
        name="diff_combine_out_proj",
    )(xall, a0, a1, lp, gs, wo)


def _round_up(n, m):
    return -(-n // m) * m


def kernel(x_prompt, x_sample, state_lru_h, state_conv, cache_k, cache_v, page_table, meta_tokens, g_mix, g_ffn, lru_w_gate, lru_w_in, lru_conv_w, lru_conv_b, lru_w_a, lru_b_a, lru_w_x, lru_b_x, lru_lambda, lru_w_out, kv_norm, w_k, w_v, k_norm, w_q, q_norm, lam_params, sub_norm, w_o, peer_w_q, peer_keys, peer_u, peer_v):
    nbatch, seq, d = x_prompt.shape
    nb_dec, dec_seq, _ = x_sample.shape
    depth = g_mix.shape[0]
    assert depth == 2 and lru_w_gate.shape[0] == 1 and w_q.shape[0] == 1
    head_dim = k_norm.shape[0]
    nheads = d // (2 * head_dim)
    assert 2 * head_dim == LANE
    past_len = page_table.shape[1] * cache_k.shape[1]
    t_real = seq + N_META
    tp = _round_up(t_real, math.lcm(LRU_TILE, ATT_TILE))
    ns = nb_dec * dec_seq
    row_s = nbatch * tp
    assert row_s % ns == 0
    nt = _round_up(row_s + ns, math.lcm(TOK_TILE, ROUTE_TILE))
    row2 = lambda a: a.reshape(1, -1)

    meta = jnp.broadcast_to(meta_tokens.astype(F32)[None], (nbatch, N_META, d))
    xp = jnp.concatenate([meta, x_prompt, jnp.zeros((nbatch, tp - t_real, d), F32)], axis=1)
    xs = x_sample.transpose(1, 0, 2).reshape(ns, d)
    xall = jnp.concatenate([xp.reshape(row_s, d), xs, jnp.zeros((nt - row_s - ns, d), F32)], 0)

    lw = [row2(g_mix[0]), lru_w_gate[0].astype(BF16), lru_w_in[0].astype(BF16), lru_conv_w[0],
          row2(lru_conv_b[0]), lru_w_a[0].astype(BF16), row2(lru_b_a[0]),
          lru_w_x[0].astype(BF16), row2(lru_b_x[0]), row2(lru_lambda[0]),
          lru_w_out[0].astype(BF16)]
    xall, h_p, buf_p = _lru_call(
        xall, jnp.zeros((nbatch, 1, d), F32), jnp.zeros((nbatch, CONV_W - 1, d), F32), lw,
        nbatch=nbatch, nseq=1, rows=LRU_TILE, ntile=tp // LRU_TILE, row0=0, reset_first=True,
        last_t=t_real - 1)
    buf0_s = state_conv[0].transpose(1, 0, 2).reshape(1, (CONV_W - 1) * nb_dec, d)
    xall, h_s, buf_s = _lru_call(
        xall, state_lru_h[0][None], buf0_s, lw,
        nbatch=1, nseq=nb_dec, rows=ns, ntile=1, row0=row_s, reset_first=False,
        last_t=dec_seq - 1)

    def peer_layer(x, layer):
        keys = peer_keys[layer].reshape(-1, peer_keys.shape[3], peer_keys.shape[4]).astype(BF16)
        nkeys = peer_keys.shape[3]
        return _peer(x, row2(g_ffn[layer]), peer_w_q[layer].T.astype(BF16), keys,
                     _slab_order(peer_u[layer].astype(BF16), nkeys),
                     _slab_order(peer_v[layer].astype(BF16), nkeys).T)

    xall = peer_layer(xall, 0)

    half = head_dim // 2
    inv = ROPE_THETA ** (-jnp.arange(half, dtype=F32) / half)
    rows = jnp.arange(nt)
    pos = jnp.where(rows < row_s, rows % tp,
                    jnp.where(rows < row_s + ns, past_len + (rows - row_s) // nb_dec, 0))
    ang = pos.astype(F32)[:, None] * inv[None, :]
    cos = jnp.tile(jnp.cos(ang), (1, LANE // half))
    sin = jnp.sin(ang)
    sin_signed = jnp.tile(jnp.concatenate([-sin, sin], axis=1), (1, LANE // head_dim))
    tile2 = lambda g: jnp.tile(g, LANE // head_dim).reshape(1, LANE)
    k_all, v_all, kb, vb, qb = _qkv_call(
        xall, cos, sin_signed, row2(kv_norm), row2(g_mix[1]), w_k.astype(BF16),
        w_v.astype(BF16), w_q[0].astype(BF16), tile2(k_norm), tile2(q_norm[0]), head_dim)

    zeros = jnp.zeros((nt, d), F32)
    a0, a1 = _prompt_attn_call(qb, kb, vb, zeros, zeros, nbatch, tp, head_dim)

    to_seq = lambda a: a[row_s:row_s + ns].reshape(dec_seq, nb_dec, d).transpose(1, 0, 2)
    npool, psz = cache_k.shape[:2]
    pad_tok = ((0, 0), (0, psz - dec_seq), (0, 0))
    knt = jnp.pad(to_seq(k_all), pad_tok).transpose(0, 2, 1)
    vn = jnp.pad(to_seq(v_all), pad_tok).reshape(nb_dec, psz * nheads, LANE)
    o_s = _decode_attn_call(
        page_table, to_seq(qb), knt, vn,
        cache_k.reshape(npool, psz, d).transpose(0, 2, 1),
        cache_v.reshape(npool, psz * nheads, LANE), head_dim)
    o_s = o_s.reshape(nb_dec, 2, dec_seq, d)
    from_seq = lambda a: a.transpose(1, 0, 2).reshape(ns, d)
    a0 = lax.dynamic_update_slice(a0, from_seq(o_s[:, 0]), (row_s, 0))
    a1 = lax.dynamic_update_slice(a1, from_seq(o_s[:, 1]), (row_s, 0))

    lambda_init = 0.8 - 0.6 * math.exp(-0.3 * 1)
    xall = _combine_call(xall, a0, a1, lam_params[0], row2(sub_norm[0]), w_o[0].astype(BF16),
                         lambda_init)
    xall = peer_layer(xall, 1)

    prompt = lambda a: a[:row_s].reshape(nbatch, tp, d)
    y_prompt = prompt(xall)[:, N_META:t_real]
    y_sample = to_seq(xall)
    k_p = prompt(k_all)[:, :t_real].reshape(nbatch, t_real, nheads, 2, head_dim)
    v_p = prompt(v_all)[:, :t_real].reshape(nbatch, t_real, nheads, 2 * head_dim)
    k_s = to_seq(k_all).reshape(nb_dec, dec_seq, nheads, 2, head_dim)
    v_s = to_seq(v_all).reshape(nb_dec, dec_seq, nheads, 2 * head_dim)
    conv_s = buf_s.reshape(CONV_W - 1, nb_dec, d).transpose(1, 0, 2)[None]
    return (y_prompt, y_sample, h_p.reshape(1, nbatch, d), buf_p[None], k_p, v_p,
            h_s, conv_s, k_s, v_s)
```

```python
import functools
import math

import jax
import jax.numpy as jnp
from jax import lax
from jax.experimental import pallas as pl
from jax.experimental.pallas import tpu as pltpu

F32 = jnp.float32
BF16 = jnp.bfloat16

EPS = 1e-6
NEG = -1e30
LRU_C = 8.0
ROPE_THETA = 10000.0
CONV_W = 4
N_META = 16
PEER_TOPK = 16

LANE = 128
SUBLANE = 8
VMEM_LIMIT = 56 * 1024 * 1024

LRU_TILE = 640
ATT_TILE = 640
TOK_TILE = 512
ROUTE_TILE = 256
EXPERT_CHUNK = 1024
GATE_BLOCK = 32
PAGES_PER_STEP = 8


def _cparams(sem):
    return pltpu.CompilerParams(dimension_semantics=sem, vmem_limit_bytes=VMEM_LIMIT)


def _rmsnorm(x, g):
    ms = jnp.mean(x * x, axis=-1, keepdims=True)
    return x * lax.rsqrt(ms + EPS) * g


def _full(shape):
    n = len(shape)
    return pl.BlockSpec(shape, lambda *_: (0,) * n)


def _lru_kernel(x_ref, h0_ref, buf0_ref, g_ref, wg_ref, wi_ref, cw_ref, cb_ref, wa_ref, ba_ref,
                wx_ref, bx_ref, lam_ref, wo_ref,
                xo_ref, ht_ref, buft_ref,
                ubuf, a_s, u_s, hs_s, h_s, *, nseq, rows, reset_first, last_t):
    ti = pl.program_id(1)
    d = x_ref.shape[-1]
    tail = (CONV_W - 1) * nseq
    off = -(-tail // SUBLANE) * SUBLANE

    @pl.when(ti == 0)
    def _():
        h_s[...] = h0_ref[0]
        ubuf[off - tail:off, :] = buf0_ref[0]

    x = x_ref[...]
    xb = _rmsnorm(x, g_ref[...]).astype(BF16)
    gate = jax.nn.gelu(jnp.dot(xb, wg_ref[...], preferred_element_type=F32))
    ubuf[off:off + rows, :] = jnp.dot(xb, wi_ref[...], preferred_element_type=F32)

    y = cb_ref[...]
    for k in range(CONV_W):
        s = off - (CONV_W - 1 - k) * nseq
        y = y + ubuf[s:s + rows, :] * cw_ref[k:k + 1, :]

    steps = rows // nseq
    lt_tile, lt_loc = divmod(last_t, steps)

    @pl.when(ti == lt_tile)
    def _():
        s = off + (lt_loc - (CONV_W - 2)) * nseq
        buft_ref[0] = ubuf[s:s + tail, :]

    ubuf[off - tail:off, :] = ubuf[off + rows - tail:off + rows, :]

    lam = -lam_ref[...]
    softplus = jnp.maximum(lam, 0.0) + jnp.log1p(jnp.exp(-jnp.abs(lam)))
    nblk = wa_ref.shape[0]
    bw = d // nblk
    if reset_first:
        row = lax.broadcasted_iota(jnp.int32, (rows, bw), 0)
        is_t0 = row < jnp.where(ti == 0, nseq, 0)
    for n in range(nblk):
        sl = slice(n * bw, (n + 1) * bw)
        yn = y[:, sl]
        ybn = yn.astype(BF16)
        r = jax.nn.sigmoid(jnp.dot(ybn, wa_ref[n], preferred_element_type=F32) + ba_ref[:, sl])
        i = jax.nn.sigmoid(jnp.dot(ybn, wx_ref[n], preferred_element_type=F32) + bx_ref[:, sl])
        log_a = -LRU_C * r * softplus[:, sl]
        a = jnp.exp(log_a)
        mult = jnp.sqrt(-jnp.tanh(log_a) * (a * a + 1.0))
        if reset_first:
            mult = jnp.where(is_t0, 1.0, mult)
        a_s[:, sl] = a
        u_s[:, sl] = mult * i * yn

    if nseq == 1:
        row8 = lax.broadcasted_iota(jnp.int32, (SUBLANE, d), 0)

        def group(gi, h):
            base = pl.multiple_of(gi * SUBLANE, SUBLANE)
            a8 = a_s[pl.ds(base, SUBLANE), :]
            u8 = u_s[pl.ds(base, SUBLANE), :]
            hs8 = jnp.zeros_like(a8)
            for r_ in range(SUBLANE):
                h = a8[r_:r_ + 1, :] * h + u8[r_:r_ + 1, :]
                hs8 = jnp.where(row8 == r_, h, hs8)
            hs_s[pl.ds(base, SUBLANE), :] = hs8
            return h
        h = lax.fori_loop(0, rows // SUBLANE, group, h_s[...])
    else:
        h = h_s[...]
        for t in range(steps):
            h = a_s[t * nseq:(t + 1) * nseq, :] * h + u_s[t * nseq:(t + 1) * nseq, :]
            hs_s[t * nseq:(t + 1) * nseq, :] = h
    h_s[...] = h

    @pl.when(ti == lt_tile)
    def _():
        ht_ref[0] = hs_s[lt_loc * nseq:(lt_loc + 1) * nseq, :]

    go = (gate * hs_s[...]).astype(BF16)
    xo_ref[...] = x + jnp.dot(go, wo_ref[...], preferred_element_type=F32)


def _lru_call(xall, h0, buf0, w, *, nbatch, nseq, rows, ntile, row0, reset_first, last_t):
    nt, d = xall.shape
    tail = (CONV_W - 1) * nseq
    off = -(-tail // SUBLANE) * SUBLANE
    blk0 = row0 // rows
    xmap = lambda b, t: (blk0 + b * ntile + t, 0)
    smap = lambda b, t: (b, 0, 0)
    kern = functools.partial(_lru_kernel, nseq=nseq, rows=rows, reset_first=reset_first,
                             last_t=last_t)
    return pl.pallas_call(
        kern,
        grid=(nbatch, ntile),
        in_specs=[pl.BlockSpec((rows, d), xmap),
                  pl.BlockSpec((1, nseq, d), smap),
                  pl.BlockSpec((1, tail, d), smap)] + [_full(a.shape) for a in w],
        out_specs=[pl.BlockSpec((rows, d), xmap),
                   pl.BlockSpec((1, nseq, d), smap),
                   pl.BlockSpec((1, tail, d), smap)],
        out_shape=[jax.ShapeDtypeStruct((nt, d), F32),
                   jax.ShapeDtypeStruct((nbatch, nseq, d), F32),
                   jax.ShapeDtypeStruct((nbatch, tail, d), F32)],
        scratch_shapes=[pltpu.VMEM((off + rows, d), F32),
                        pltpu.VMEM((rows, d), F32),
                        pltpu.VMEM((rows, d), F32),
                        pltpu.VMEM((rows, d), F32),
                        pltpu.VMEM((nseq, d), F32)],
        input_output_aliases={0: 0},
        compiler_params=_cparams(("arbitrary", "arbitrary")),
        name="rg_lru_block",
    )(xall, h0, buf0, *w)


def _top_values(v, n, masked=0):
    tops = []
    for _ in range(n):
        m = jnp.max(v, axis=0, keepdims=True)
        tops.append(m)
        v = jnp.where(v == m, -jnp.inf, v)
    removed = jnp.sum(jnp.where(v == -jnp.inf, 1.0, 0.0), axis=0, keepdims=True) - float(masked)
    return tops, removed


def _candidate_sums(a, b):
    k = len(a)
    assert k == 2 * SUBLANE
    t = a[0].shape[1]
    row16 = lax.broadcasted_iota(jnp.int32, (k, t), 0)
    row8 = lax.broadcasted_iota(jnp.int32, (SUBLANE, t), 0)
    b16 = jnp.zeros((k, t), F32)
    a_hi = jnp.zeros((SUBLANE, t), F32)
    for i in range(k):
        b16 = jnp.where(row16 == i, b[i], b16)
    for i in range(SUBLANE):
        a_hi = jnp.where(row8 == i, a[SUBLANE + i], a_hi)
    b8 = b16[:SUBLANE]
    parts = [a[0] + b16]
    masked = 0
    for p in range(1, SUBLANE):
        limit = k // (p + 1)
        parts.append(jnp.where(row8 < limit, a[p] + b8, -jnp.inf))
        masked += SUBLANE - limit
    parts.append(a_hi + b[0])
    return jnp.concatenate(parts, axis=0), masked


def _route_kernel(x_ref, g_ref, wqt_ref, keys_ref,
                  xnt_ref, s1_ref, c1_ref, s2_ref, e2_ref, tau_ref, flag_ref, q_s):
    nheads = s1_ref.shape[0]
    nkeys = s1_ref.shape[1]
    xn = _rmsnorm(x_ref[...], g_ref[...])
    xnt = xn.T.astype(BF16)
    xnt_ref[...] = xnt
    q_s[...] = jnp.dot(wqt_ref[...], xnt, preferred_element_type=F32)

    def head(h, carry):
        tops, dup, expd, sc = [], [], [], []
        for c in range(2):
            r0 = pl.multiple_of((h * 2 + c) * nkeys, nkeys)
            qhc = q_s[pl.ds(r0, nkeys), :].astype(BF16)
            s = jnp.dot(keys_ref[h * 2 + c], qhc, preferred_element_type=F32)
            t, removed = _top_values(s, PEER_TOPK + 1)
            tops.append(t)
            dup.append(removed != float(PEER_TOPK + 1))
            expd.append(jnp.exp(s - t[0]))
            sc.append(s)
        cand, masked = _candidate_sums(tops[0][:PEER_TOPK], tops[1][:PEER_TOPK])
        best, removed = _top_values(cand, PEER_TOPK, masked)
        tau = best[PEER_TOPK - 1]
        z = jnp.ones_like(tau)
        for t in range(1, PEER_TOPK):
            z = z + jnp.exp(best[t] - best[0])
        tie = dup[0] | dup[1] | (removed != float(PEER_TOPK))
        tie = tie | (tops[0][PEER_TOPK] + tops[1][0] >= tau) | (tops[0][0] + tops[1][PEER_TOPK] >= tau)
        s1_ref[h] = sc[0]
        c1_ref[h] = expd[0] / z
        s2_ref[h] = sc[1]
        e2_ref[h] = expd[1]
        tau_ref[h] = tau
        flag_ref[h] = jnp.where(tie, 1.0, 0.0)
        return carry

    lax.fori_loop(0, nheads, head, 0)


def _route_call(xall, g, wqt, keys):
    nt, d = xall.shape
    nhc, nkeys, _ = keys.shape
    nheads = nhc // 2
    tm = ROUTE_TILE
    tok = lambda i: (0, 0, i)
    big = jax.ShapeDtypeStruct((nheads, nkeys, nt), F32)
    small = jax.ShapeDtypeStruct((nheads, 1, nt), F32)
    return pl.pallas_call(
        _route_kernel,
        grid=(nt // tm,),
        in_specs=[pl.BlockSpec((tm, d), lambda i: (i, 0)), _full(g.shape), _full(wqt.shape),
                  _full(keys.shape)],
        out_specs=[pl.BlockSpec((d, tm), lambda i: (0, i))]
        + [pl.BlockSpec((nheads, nkeys, tm), tok)] * 4
        + [pl.BlockSpec((nheads, 1, tm), tok)] * 2,
        out_shape=[jax.ShapeDtypeStruct((d, nt), BF16), big, big, big, big, small, small],
        scratch_shapes=[pltpu.VMEM((wqt.shape[0], tm), F32)],
        compiler_params=_cparams(("arbitrary",)),
        name="peer_route",
    )(xall, g, wqt, keys)


def _expert_kernel(x_ref, xnt_ref, s1_ref, c1_ref, s2_ref, e2_ref, tau_ref, u_ref, vt_ref,
                   xo_ref, acc, a_s, w_s, *, jblk):
    c = pl.program_id(1)
    nheads, nkeys, tm = s2_ref.shape
    ec = u_ref.shape[0]
    nrow_i = ec // nkeys
    njb = nkeys // jblk
    slab = nrow_i * jblk
    mm_lanes = 2 * LANE
    i0 = pl.multiple_of(c * nrow_i, SUBLANE)

    @pl.when(c == 0)
    def _():
        acc[...] = jnp.zeros_like(acc)

    def gates(l0, jb):
        lanes = pl.ds(l0, LANE)
        jrows = slice(jb * jblk, (jb + 1) * jblk)
        g = [jnp.zeros((jblk, LANE), F32) for _ in range(nrow_i)]
        for h in range(nheads):
            s1 = s1_ref[h, pl.ds(i0, SUBLANE), lanes]
            c1 = c1_ref[h, pl.ds(i0, SUBLANE), lanes]
            s2 = s2_ref[h, jrows, lanes]
            e2 = e2_ref[h, jrows, lanes]
            tau = tau_ref[h, :, lanes]
            for ii in range(nrow_i):
                sel = (s1[ii:ii + 1, :] + s2) >= tau
                g[ii] = g[ii] + jnp.where(sel, e2 * c1[ii:ii + 1, :], 0.0)
        for ii in range(nrow_i):
            rows = slice(jb * slab + ii * jblk, jb * slab + (ii + 1) * jblk)
            w_s[rows, lanes] = (g[ii] * jax.nn.gelu(a_s[rows, lanes])).astype(BF16)

    def piece(l, carry):
        m0 = pl.multiple_of(l * mm_lanes, mm_lanes)
        lanes = pl.ds(m0, mm_lanes)

        for jb in range(njb):
            rows = slice(jb * slab, (jb + 1) * slab)
            a_s[rows, lanes] = jnp.dot(u_ref[rows, :], xnt_ref[:, lanes],
                                       preferred_element_type=F32)
            for sub in range(mm_lanes // LANE):
                gates(pl.multiple_of(m0 + sub * LANE, LANE), jb)
            acc[:, lanes] += jnp.dot(vt_ref[:, rows], w_s[rows, lanes],
                                     preferred_element_type=F32)
        return carry

    lax.fori_loop(0, tm // mm_lanes, piece, 0)

    @pl.when(c == pl.num_programs(1) - 1)
    def _():
        xo_ref[...] = x_ref[...] + acc[...].T


def _expert_call(xall, xnt, s1, c1, s2, e2, tau, u, vt):
    nt, d = xall.shape
    nheads, nkeys, _ = s1.shape
    nexp = u.shape[0]
    tm, ec = TOK_TILE, EXPERT_CHUNK
    tok = lambda i, c: (0, 0, i)
    return pl.pallas_call(
        functools.partial(_expert_kernel, jblk=GATE_BLOCK),
        grid=(nt // tm, nexp // ec),
        in_specs=[pl.BlockSpec((tm, d), lambda i, c: (i, 0)),
                  pl.BlockSpec((d, tm), lambda i, c: (0, i))]
        + [pl.BlockSpec((nheads, nkeys, tm), tok)] * 4
        + [pl.BlockSpec((nheads, 1, tm), tok),
           pl.BlockSpec((ec, d), lambda i, c: (c, 0)),
           pl.BlockSpec((d, ec), lambda i, c: (0, c))],
        out_specs=pl.BlockSpec((tm, d), lambda i, c: (i, 0)),
        out_shape=jax.ShapeDtypeStruct((nt, d), F32),
        scratch_shapes=[pltpu.VMEM((d, tm), F32), pltpu.VMEM((ec, tm), F32),
                        pltpu.VMEM((ec, tm), BF16)],
        input_output_aliases={0: 0},
        compiler_params=_cparams(("arbitrary", "arbitrary")),
        name="peer_experts",
    )(xall, xnt, s1, c1, s2, e2, tau, u, vt)


def _slab_order(table, nkeys):
    nexp, d = table.shape
    nrow_i = EXPERT_CHUNK // nkeys
    t = table.reshape(nexp // EXPERT_CHUNK, nrow_i, nkeys // GATE_BLOCK, GATE_BLOCK, d)
    return t.transpose(0, 2, 1, 3, 4).reshape(nexp, d)


def _peer(xall, g, wqt, keys, u, vt):
    xnt, s1, c1, s2, e2, tau, _ = _route_call(xall, g, wqt, keys)
    return _expert_call(xall, xnt, s1, c1, s2, e2, tau, u, vt)


def _exact_group_sum(sq, ones_bd):
    hi = sq.astype(BF16)
    r1 = sq - hi.astype(F32)
    mid = r1.astype(BF16)
    lo = (r1 - mid.astype(F32)).astype(BF16)
    dot = lambda t: jnp.dot(t, ones_bd, preferred_element_type=F32)
    return dot(hi) + dot(mid) + dot(lo)


def _head_norm_rope(y, g2, cos, sin_signed, ones_bd, first_half, head_dim):
    out = []
    for n in range(y.shape[1] // LANE):
        blk = y[:, n * LANE:(n + 1) * LANE]
        ms = _exact_group_sum(blk * blk, ones_bd) / head_dim
        z = blk * lax.rsqrt(ms + EPS) * g2
        partner = jnp.where(first_half, pltpu.roll(z, LANE - head_dim // 2, 1),
                            pltpu.roll(z, head_dim // 2, 1))
        out.append(z * cos + partner * sin_signed)
    return jnp.concatenate(out, axis=1)


def _qkv_kernel(x_ref, cos_ref, sin_ref, gkv_ref, gq_ref, wk_ref, wv_ref, wq_ref, gk2_ref,
                gq2_ref, k_ref, v_ref, kb_ref, vt_ref, qt_ref, qb_ref, *, head_dim):
    x = x_ref[...]
    cos = cos_ref[...]
    sin_signed = sin_ref[...]
    lane = lax.broadcasted_iota(jnp.int32, (1, LANE), 1)
    first_half = (lane % head_dim) < head_dim // 2
    r = lax.broadcasted_iota(jnp.int32, (LANE, LANE), 0) // head_dim
    cidx = lax.broadcasted_iota(jnp.int32, (LANE, LANE), 1) // head_dim
    ones_bd = jnp.where(r == cidx, 1.0, 0.0).astype(BF16)

    hk = _rmsnorm(x, gkv_ref[...]).astype(BF16)
    k = _head_norm_rope(jnp.dot(hk, wk_ref[...], preferred_element_type=F32), gk2_ref[...],
                        cos, sin_signed, ones_bd, first_half, head_dim)
    v = jnp.dot(hk, wv_ref[...], preferred_element_type=F32)
    hq = _rmsnorm(x, gq_ref[...]).astype(BF16)
    q = _head_norm_rope(jnp.dot(hq, wq_ref[...], preferred_element_type=F32), gq2_ref[...],
                        cos, sin_signed, ones_bd, first_half, head_dim)
    k_ref[...] = k
    v_ref[...] = v
    kb_ref[...] = k.astype(BF16)
    vt_ref[...] = v.T.astype(BF16)
    qs = q * head_dim ** -0.5
    qt_ref[...] = qs.T.astype(BF16)
    qb_ref[...] = qs.astype(BF16)


def _qkv_call(xall, cos, sin_signed, gkv, gq, wk, wv, wq, gk2, gq2, head_dim):
    nt, d = xall.shape
    tm = TOK_TILE
    row = lambda i: (i, 0)
    ws = [gkv, gq, wk, wv, wq, gk2, gq2]
    return pl.pallas_call(
        functools.partial(_qkv_kernel, head_dim=head_dim),
        grid=(nt // tm,),
        in_specs=[pl.BlockSpec((tm, d), row), pl.BlockSpec((tm, LANE), row),
                  pl.BlockSpec((tm, LANE), row)] + [_full(a.shape) for a in ws],
        out_specs=[pl.BlockSpec((tm, d), row)] * 3 + [pl.BlockSpec((d, tm), lambda i: (0, i))] * 2
        + [pl.BlockSpec((tm, d), row)],
        out_shape=[jax.ShapeDtypeStruct((nt, d), F32)] * 2 + [jax.ShapeDtypeStruct((nt, d), BF16)]
        + [jax.ShapeDtypeStruct((d, nt), BF16)] * 2 + [jax.ShapeDtypeStruct((nt, d), BF16)],
        compiler_params=_cparams(("arbitrary",)),
        name="qkv_proj",
    )(xall, cos, sin_signed, *ws)


def _prompt_attn_kernel(qt_ref, k_ref, vt_ref, z0_ref, z1_ref, a0_ref, a1_ref, *, head_dim):
    del z0_ref, z1_ref
    qi = pl.program_id(2)
    tq = qt_ref.shape[1]
    tk = tq
    qt = qt_ref[...]
    feat = lax.broadcasted_iota(jnp.int32, (LANE, 1), 0)
    zero = jnp.zeros_like(qt)
    qs = jnp.concatenate([jnp.where(feat < head_dim, qt, zero),
                          jnp.where(feat >= head_dim, qt, zero)], axis=1)

    def update(carry, j, masked):
        m, l, acc = carry
        k0 = pl.multiple_of(j * tk, tk)
        kj = k_ref[pl.ds(k0, tk), :]
        vj = vt_ref[:, pl.ds(k0, tk)]
        s = jnp.dot(kj, qs, preferred_element_type=F32)
        if masked:
            kpos = lax.broadcasted_iota(jnp.int32, (tk, 2 * tq), 0)
            qpos = lax.broadcasted_iota(jnp.int32, (tk, 2 * tq), 1) % tq
            s = jnp.where(kpos <= qpos, s, NEG)
        m_new = jnp.maximum(m, jnp.max(s, axis=0, keepdims=True))
        corr = jnp.exp(m - m_new)
        p = jnp.exp(s - m_new)
        l = l * corr + jnp.sum(p, axis=0, keepdims=True)
        acc = acc * corr + jnp.dot(vj, p.astype(BF16), preferred_element_type=F32)
        return m_new, l, acc

    init = (jnp.full((1, 2 * tq), NEG, F32), jnp.zeros((1, 2 * tq), F32),
            jnp.zeros((LANE, 2 * tq), F32))
    carry = lax.fori_loop(0, qi, lambda j, cr: update(cr, j, False), init)
    m, l, acc = update(carry, qi, True)
    out = acc / l
    a0_ref[...] = out[:, :tq]
    a1_ref[...] = out[:, tq:]


def _prompt_attn_call(qt, kb, vt, a0t, a1t, nbatch, tp, head_dim):
    d, nt = qt.shape
    nheads = d // LANE
    tq = ATT_TILE
    nq = tp // tq
    qmap = lambda b, h, i: (h, b * nq + i)
    return pl.pallas_call(
        functools.partial(_prompt_attn_kernel, head_dim=head_dim),
        grid=(nbatch, nheads, nq),
        in_specs=[pl.BlockSpec((LANE, tq), qmap),
                  pl.BlockSpec((tp, LANE), lambda b, h, i: (b, h)),
                  pl.BlockSpec((LANE, tp), lambda b, h, i: (h, b)),
                  pl.BlockSpec(memory_space=pl.ANY), pl.BlockSpec(memory_space=pl.ANY)],
        out_specs=[pl.BlockSpec((LANE, tq), qmap)] * 2,
        out_shape=[jax.ShapeDtypeStruct((d, nt), F32)] * 2,
        input_output_aliases={3: 0, 4: 1},
        compiler_params=_cparams(("arbitrary", "arbitrary", "arbitrary")),
        name="prompt_attention",
    )(qt, kb, vt, a0t, a1t)


def _decode_attn_kernel(pt_ref, q_ref, kn_ref, vn_ref, *refs, head_dim, npg):
    del pt_ref
    kp = refs[:npg]
    vp = refs[npg:2 * npg]
    o_ref, qblk, expand, hmask, m_s, l_s, acc = refs[2 * npg:]
    p = pl.program_id(1)
    nrow, d = qblk.shape
    nheads = d // LANE
    ntok = nrow // (2 * nheads)
    psz = expand.shape[0]

    @pl.when(p == 0)
    def _():
        q = q_ref[0].astype(F32)
        row = lax.broadcasted_iota(jnp.int32, (nrow, d), 0)
        col = lax.broadcasted_iota(jnp.int32, (nrow, d), 1)
        rep = jnp.concatenate(
            [jnp.broadcast_to(q[t:t + 1, :], (nheads, d)) for t in range(ntok)] * 2, axis=0)
        mine = (col // LANE == row % nheads) & ((col % LANE) // head_dim == row // (ntok * nheads))
        qblk[...] = jnp.where(mine, rep, 0.0).astype(BF16)
        pos = lax.broadcasted_iota(jnp.int32, expand.shape, 0)
        ecol = lax.broadcasted_iota(jnp.int32, expand.shape, 1)
        expand[...] = jnp.where(ecol // nheads == pos, 1.0, 0.0).astype(BF16)
        hrow = lax.broadcasted_iota(jnp.int32, hmask.shape, 0)
        hcol = lax.broadcasted_iota(jnp.int32, hmask.shape, 1)
        hmask[...] = jnp.where(hcol % nheads == hrow % nheads, 1.0, 0.0)
        m_s[...] = jnp.full_like(m_s, NEG)
        l_s[...] = jnp.zeros_like(l_s)
        acc[...] = jnp.zeros_like(acc)

    def update(s, vs):
        m = m_s[...]
        m_new = jnp.maximum(m, jnp.max(s, axis=-1, keepdims=True))
        corr = jnp.exp(m - m_new)
        pr = jnp.exp(s - m_new)
        l_s[...] = l_s[...] * corr + jnp.sum(pr, axis=-1, keepdims=True)
        m_s[...] = m_new
        pv = acc[...] * corr
        for i, v in enumerate(vs):
            spread = jnp.dot(pr[:, i * psz:(i + 1) * psz].astype(BF16), expand[...],
                             preferred_element_type=F32) * hmask[...]
            pv = pv + jnp.dot(spread.astype(BF16), v, preferred_element_type=F32)
        acc[...] = pv

    qb = qblk[...]
    s = jnp.concatenate([jnp.dot(qb, kp[i][0].astype(BF16), preferred_element_type=F32)
                         for i in range(npg)], axis=1)
    update(s, [vp[i][0].astype(BF16) for i in range(npg)])

    @pl.when(p == pl.num_programs(1) - 1)
    def _():
        s = jnp.dot(qblk[...], kn_ref[0].astype(BF16), preferred_element_type=F32)
        trow = (lax.broadcasted_iota(jnp.int32, s.shape, 0) // nheads) % ntok
        tcol = lax.broadcasted_iota(jnp.int32, s.shape, 1)
        update(jnp.where(tcol <= trow, s, NEG), [vn_ref[0].astype(BF16)])
        o_ref[0] = acc[...] / l_s[...]


def _decode_attn_call(page_table, q, knt, vn, cache_kt, cache_v, head_dim):
    nb, ntok, d = q.shape
    npages = page_table.shape[1]
    npg = PAGES_PER_STEP
    psz = cache_kt.shape[2]
    nheads = d // LANE
    nrow = 2 * ntok * nheads
    bmap = lambda b, p, pt: (b, 0, 0)
    pmaps = [functools.partial(lambda b, p, pt, i: (pt[b, p * npg + i], 0, 0), i=i)
             for i in range(npg)]
    grid_spec = pltpu.PrefetchScalarGridSpec(
        num_scalar_prefetch=1,
        grid=(nb, npages // npg),
        in_specs=[pl.BlockSpec((1, ntok, d), bmap), pl.BlockSpec((1, d, psz), bmap),
                  pl.BlockSpec((1, psz * nheads, LANE), bmap)]
        + [pl.BlockSpec((1, d, psz), pm) for pm in pmaps]
        + [pl.BlockSpec((1, psz * nheads, LANE), pm) for pm in pmaps],
        out_specs=pl.BlockSpec((1, nrow, LANE), bmap),
        scratch_shapes=[pltpu.VMEM((nrow, d), BF16),
                        pltpu.VMEM((psz, psz * nheads), BF16),
                        pltpu.VMEM((nrow, psz * nheads), F32),
                        pltpu.VMEM((nrow, 1), F32),
                        pltpu.VMEM((nrow, 1), F32),
                        pltpu.VMEM((nrow, LANE), F32)])
    return pl.pallas_call(
        functools.partial(_decode_attn_kernel, head_dim=head_dim, npg=npg),
        grid_spec=grid_spec,
        out_shape=jax.ShapeDtypeStruct((nb, nrow, LANE), F32),
        compiler_params=_cparams(("arbitrary", "arbitrary")),
        name="decode_attention",
    )(page_table, q, knt, vn, *([cache_kt] * npg), *([cache_v] * npg))


def _combine_kernel(x_ref, a0_ref, a1_ref, lp_ref, gs_ref, wot_ref, xo_ref, *, lambda_init):
    lp = lp_ref[...]
    lam = (jnp.exp(jnp.sum(lp[0:1] * lp[1:2], axis=-1, keepdims=True))
           - jnp.exp(jnp.sum(lp[2:3] * lp[3:4], axis=-1, keepdims=True)) + lambda_init)
    o = a0_ref[...] - lam * a1_ref[...]
    parts = []
    for h in range(o.shape[0] // LANE):
        oh = o[h * LANE:(h + 1) * LANE, :]
        ms = jnp.mean(oh * oh, axis=0, keepdims=True)
        parts.append(oh * lax.rsqrt(ms + EPS) * gs_ref[...] * (1.0 - lambda_init))
    ob = jnp.concatenate(parts, axis=0).astype(BF16)
    yt = jnp.dot(wot_ref[...], ob, preferred_element_type=F32)
    xo_ref[...] = x_ref[...] + yt.T


def _combine_call(xall, a0t, a1t, lp, gs, wot, lambda_init):
    nt, d = xall.shape
    tm = TOK_TILE
    row = lambda i: (i, 0)
    col = lambda i: (0, i)
    return pl.pallas_call(
        functools.partial(_combine_kernel, lambda_init=lambda_init),
        grid=(nt // tm,),
        in_specs=[pl.BlockSpec((tm, d), row), pl.BlockSpec((d, tm), col),
                  pl.BlockSpec((d, tm), col), _full(lp.shape), _full(gs.shape),
                  _full(wot.shape)],
        out_specs=pl.BlockSpec((tm, d), row),
        out_shape=jax.ShapeDtypeStruct((nt, d), F32),
        input_output_aliases={0: 0},
        compiler_params=_cparams(("arbitrary",)),
        name="diff_combine_out_proj",
    )(xall, a0t, a1t, lp, gs, wot)


def _round_up(n, m):
    return -(-n // m) * m


def kernel(x_prompt, x_sample, state_lru_h, state_conv, cache_k, cache_v, page_table, meta_tokens, g_mix, g_ffn, lru_w_gate, lru_w_in, lru_conv_w, lru_conv_b, lru_w_a, lru_b_a, lru_w_x, lru_b_x, lru_lambda, lru_w_out, kv_norm, w_k, w_v, k_norm, w_q, q_norm, lam_params, sub_norm, w_o, peer_w_q, peer_keys, peer_u, peer_v):
    nbatch, seq, d = x_prompt.shape
    nb_dec, dec_seq, _ = x_sample.shape
    depth = g_mix.shape[0]
    assert depth == 2 and lru_w_gate.shape[0] == 1 and w_q.shape[0] == 1
    head_dim = k_norm.shape[0]
    nheads = d // (2 * head_dim)
    assert 2 * head_dim == LANE
    past_len = page_table.shape[1] * cache_k.shape[1]
    t_real = seq + N_META
    tp = _round_up(t_real, math.lcm(LRU_TILE, ATT_TILE))
    ns = nb_dec * dec_seq
    row_s = nbatch * tp
    assert row_s % ns == 0
    nt = _round_up(row_s + ns, math.lcm(TOK_TILE, ROUTE_TILE))
    row2 = lambda a: a.reshape(1, -1)

    meta = jnp.broadcast_to(meta_tokens.astype(F32)[None], (nbatch, N_META, d))
    xp = jnp.concatenate([meta, x_prompt, jnp.zeros((nbatch, tp - t_real, d), F32)], axis=1)
    xs = x_sample.transpose(1, 0, 2).reshape(ns, d)
    xall = jnp.concatenate([xp.reshape(row_s, d), xs, jnp.zeros((nt - row_s - ns, d), F32)], 0)

    lw = [row2(g_mix[0]), lru_w_gate[0].astype(BF16), lru_w_in[0].astype(BF16), lru_conv_w[0],
          row2(lru_conv_b[0]), lru_w_a[0].astype(BF16), row2(lru_b_a[0]),
          lru_w_x[0].astype(BF16), row2(lru_b_x[0]), row2(lru_lambda[0]),
          lru_w_out[0].astype(BF16)]
    xall, h_p, buf_p = _lru_call(
        xall, jnp.zeros((nbatch, 1, d), F32), jnp.zeros((nbatch, CONV_W - 1, d), F32), lw,
        nbatch=nbatch, nseq=1, rows=LRU_TILE, ntile=tp // LRU_TILE, row0=0, reset_first=True,
        last_t=t_real - 1)
    buf0_s = state_conv[0].transpose(1, 0, 2).reshape(1, (CONV_W - 1) * nb_dec, d)
    xall, h_s, buf_s = _lru_call(
        xall, state_lru_h[0][None], buf0_s, lw,
        nbatch=1, nseq=nb_dec, rows=ns, ntile=1, row0=row_s, reset_first=False,
        last_t=dec_seq - 1)

    def peer_layer(x, layer):
        keys = peer_keys[layer].reshape(-1, peer_keys.shape[3], peer_keys.shape[4]).astype(BF16)
        nkeys = peer_keys.shape[3]
        return _peer(x, row2(g_ffn[layer]), peer_w_q[layer].T.astype(BF16), keys,
                     _slab_order(peer_u[layer].astype(BF16), nkeys),
                     _slab_order(peer_v[layer].astype(BF16), nkeys).T)

    xall = peer_layer(xall, 0)

    half = head_dim // 2
    inv = ROPE_THETA ** (-jnp.arange(half, dtype=F32) / half)
    rows = jnp.arange(nt)
    pos = jnp.where(rows < row_s, rows % tp,
                    jnp.where(rows < row_s + ns, past_len + (rows - row_s) // nb_dec, 0))
    ang = pos.astype(F32)[:, None] * inv[None, :]
    cos = jnp.tile(jnp.cos(ang), (1, LANE // half))
    sin = jnp.sin(ang)
    sin_signed = jnp.tile(jnp.concatenate([-sin, sin], axis=1), (1, LANE // head_dim))
    tile2 = lambda g: jnp.tile(g, LANE // head_dim).reshape(1, LANE)
    k_all, v_all, kb, vt, qt, qb = _qkv_call(
        xall, cos, sin_signed, row2(kv_norm), row2(g_mix[1]), w_k.astype(BF16),
        w_v.astype(BF16), w_q[0].astype(BF16), tile2(k_norm), tile2(q_norm[0]), head_dim)

    zeros = jnp.zeros((d, nt), F32)
    a0t, a1t = _prompt_attn_call(qt, kb, vt, zeros, zeros, nbatch, tp, head_dim)

    to_seq = lambda a: a[row_s:row_s + ns].reshape(dec_seq, nb_dec, d).transpose(1, 0, 2)
    npool, psz = cache_k.shape[:2]
    pad_tok = ((0, 0), (0, psz - dec_seq), (0, 0))
    knt = jnp.pad(to_seq(k_all), pad_tok).transpose(0, 2, 1)
    vn = jnp.pad(to_seq(v_all), pad_tok).reshape(nb_dec, psz * nheads, LANE)
    o_s = _decode_attn_call(
        page_table, to_seq(qb), knt, vn,
        cache_k.reshape(npool, psz, d).transpose(0, 2, 1),
        cache_v.reshape(npool, psz * nheads, LANE), head_dim)
    o_s = o_s.reshape(nb_dec, 2, dec_seq, d)
    from_seq_t = lambda a: a.transpose(2, 1, 0).reshape(d, ns)
    a0t = lax.dynamic_update_slice(a0t, from_seq_t(o_s[:, 0]), (0, row_s))
    a1t = lax.dynamic_update_slice(a1t, from_seq_t(o_s[:, 1]), (0, row_s))

    lambda_init = 0.8 - 0.6 * math.exp(-0.3 * 1)
    xall = _combine_call(xall, a0t, a1t, lam_params[0], sub_norm[0].reshape(-1, 1),
                         w_o[0].T.astype(BF16), lambda_init)
    xall = peer_layer(xall, 1)

    prompt = lambda a: a[:row_s].reshape(nbatch, tp, d)
    y_prompt = prompt(xall)[:, N_META:t_real]
    y_sample = to_seq(xall)
    k_p = prompt(k_all)[:, :t_real].reshape(nbatch, t_real, nheads, 2, head_dim)
    v_p = prompt(v_all)[:, :t_real].reshape(nbatch, t_real, nheads, 2 * head_dim)
    k_s = to_seq(k_all).reshape(nb_dec, dec_seq, nheads, 2, head_dim)
    v_s = to_seq(v_all).reshape(nb_dec, dec_seq, nheads, 2 * head_dim)
    conv_s = buf_s.reshape(CONV_W - 1, nb_dec, d).transpose(1, 0, 2)[None]
    return (y_prompt, y_sample, h_p.reshape(1, nbatch, d), buf_p[None], k_p, v_p,
            h_s, conv_s, k_s, v_s)
```

```python
import functools
import math

import jax
import jax.numpy as jnp
from jax import lax
from jax.experimental import pallas as pl
from jax.experimental.pallas import tpu as pltpu

F32 = jnp.float32
BF16 = jnp.bfloat16

EPS = 1e-6
NEG = -1e30
LRU_C = 8.0
ROPE_THETA = 10000.0
CONV_W = 4
N_META = 16
PEER_TOPK = 16

LANE = 128
SUBLANE = 8
VMEM_LIMIT = 56 * 1024 * 1024

LRU_TILE = 640
ATT_TILE = 640
TOK_TILE = 512
ROUTE_TILE = 256
EXPERT_CHUNK = 1024
GATE_BLOCK = 32
GATE_ROWS = 4
PAGES_PER_STEP = 8


def _cparams(sem):
    return pltpu.CompilerParams(dimension_semantics=sem, vmem_limit_bytes=VMEM_LIMIT)


def _rmsnorm(x, g):
    ms = jnp.mean(x * x, axis=-1, keepdims=True)
    return x * lax.rsqrt(ms + EPS) * g


def _full(shape):
    n = len(shape)
    return pl.BlockSpec(shape, lambda *_: (0,) * n)


def _lru_kernel(x_ref, h0_ref, buf0_ref, g_ref, wg_ref, wi_ref, cw_ref, cb_ref, wa_ref, ba_ref,
                wx_ref, bx_ref, lam_ref, wo_ref,
                xo_ref, ht_ref, buft_ref,
                ubuf, a_s, u_s, hs_s, h_s, *, nseq, rows, reset_first, last_t):
    ti = pl.program_id(1)
    d = x_ref.shape[-1]
    tail = (CONV_W - 1) * nseq
    off = -(-tail // SUBLANE) * SUBLANE

    @pl.when(ti == 0)
    def _():
        h_s[...] = h0_ref[0]
        ubuf[off - tail:off, :] = buf0_ref[0]

    x = x_ref[...]
    xb = _rmsnorm(x, g_ref[...]).astype(BF16)
    gate = jax.nn.gelu(jnp.dot(xb, wg_ref[...], preferred_element_type=F32))
    ubuf[off:off + rows, :] = jnp.dot(xb, wi_ref[...], preferred_element_type=F32)

    y = cb_ref[...]
    for k in range(CONV_W):
        s = off - (CONV_W - 1 - k) * nseq
        y = y + ubuf[s:s + rows, :] * cw_ref[k:k + 1, :]

    steps = rows // nseq
    lt_tile, lt_loc = divmod(last_t, steps)

    @pl.when(ti == lt_tile)
    def _():
        s = off + (lt_loc - (CONV_W - 2)) * nseq
        buft_ref[0] = ubuf[s:s + tail, :]

    ubuf[off - tail:off, :] = ubuf[off + rows - tail:off + rows, :]

    lam = -lam_ref[...]
    softplus = jnp.maximum(lam, 0.0) + jnp.log1p(jnp.exp(-jnp.abs(lam)))
    nblk = wa_ref.shape[0]
    bw = d // nblk
    if reset_first:
        row = lax.broadcasted_iota(jnp.int32, (rows, bw), 0)
        is_t0 = row < jnp.where(ti == 0, nseq, 0)
    for n in range(nblk):
        sl = slice(n * bw, (n + 1) * bw)
        yn = y[:, sl]
        ybn = yn.astype(BF16)
        r = jax.nn.sigmoid(jnp.dot(ybn, wa_ref[n], preferred_element_type=F32) + ba_ref[:, sl])
        i = jax.nn.sigmoid(jnp.dot(ybn, wx_ref[n], preferred_element_type=F32) + bx_ref[:, sl])
        log_a = -LRU_C * r * softplus[:, sl]
        a = jnp.exp(log_a)
        mult = jnp.sqrt(-jnp.tanh(log_a) * (a * a + 1.0))
        if reset_first:
            mult = jnp.where(is_t0, 1.0, mult)
        a_s[:, sl] = a
        u_s[:, sl] = mult * i * yn

    if nseq == 1:
        row8 = lax.broadcasted_iota(jnp.int32, (SUBLANE, d), 0)

        def group(gi, h):
            base = pl.multiple_of(gi * SUBLANE, SUBLANE)
            a8 = a_s[pl.ds(base, SUBLANE), :]
            u8 = u_s[pl.ds(base, SUBLANE), :]
            hs8 = jnp.zeros_like(a8)
            for r_ in range(SUBLANE):
                h = a8[r_:r_ + 1, :] * h + u8[r_:r_ + 1, :]
                hs8 = jnp.where(row8 == r_, h, hs8)
            hs_s[pl.ds(base, SUBLANE), :] = hs8
            return h
        h = lax.fori_loop(0, rows // SUBLANE, group, h_s[...])
    else:
        h = h_s[...]
        for t in range(steps):
            h = a_s[t * nseq:(t + 1) * nseq, :] * h + u_s[t * nseq:(t + 1) * nseq, :]
            hs_s[t * nseq:(t + 1) * nseq, :] = h
    h_s[...] = h

    @pl.when(ti == lt_tile)
    def _():
        ht_ref[0] = hs_s[lt_loc * nseq:(lt_loc + 1) * nseq, :]

    go = (gate * hs_s[...]).astype(BF16)
    xo_ref[...] = x + jnp.dot(go, wo_ref[...], preferred_element_type=F32)


def _lru_call(xall, h0, buf0, w, *, nbatch, nseq, rows, ntile, row0, reset_first, last_t):
    nt, d = xall.shape
    tail = (CONV_W - 1) * nseq
    off = -(-tail // SUBLANE) * SUBLANE
    blk0 = row0 // rows
    xmap = lambda b, t: (blk0 + b * ntile + t, 0)
    smap = lambda b, t: (b, 0, 0)
    kern = functools.partial(_lru_kernel, nseq=nseq, rows=rows, reset_first=reset_first,
                             last_t=last_t)
    return pl.pallas_call(
        kern,
        grid=(nbatch, ntile),
        in_specs=[pl.BlockSpec((rows, d), xmap),
                  pl.BlockSpec((1, nseq, d), smap),
                  pl.BlockSpec((1, tail, d), smap)] + [_full(a.shape) for a in w],
        out_specs=[pl.BlockSpec((rows, d), xmap),
                   pl.BlockSpec((1, nseq, d), smap),
                   pl.BlockSpec((1, tail, d), smap)],
        out_shape=[jax.ShapeDtypeStruct((nt, d), F32),
                   jax.ShapeDtypeStruct((nbatch, nseq, d), F32),
                   jax.ShapeDtypeStruct((nbatch, tail, d), F32)],
        scratch_shapes=[pltpu.VMEM((off + rows, d), F32),
                        pltpu.VMEM((rows, d), F32),
                        pltpu.VMEM((rows, d), F32),
                        pltpu.VMEM((rows, d), F32),
                        pltpu.VMEM((nseq, d), F32)],
        input_output_aliases={0: 0},
        compiler_params=_cparams(("arbitrary", "arbitrary")),
        name="rg_lru_block",
    )(xall, h0, buf0, *w)


def _top_values(v, n, masked=0):
    tops = []
    for _ in range(n):
        m = jnp.max(v, axis=0, keepdims=True)
        tops.append(m)
        v = jnp.where(v == m, -jnp.inf, v)
    removed = jnp.sum(jnp.where(v == -jnp.inf, 1.0, 0.0), axis=0, keepdims=True) - float(masked)
    return tops, removed


def _candidate_sums(a, b):
    k = len(a)
    assert k == 2 * SUBLANE
    t = a[0].shape[1]
    row16 = lax.broadcasted_iota(jnp.int32, (k, t), 0)
    row8 = lax.broadcasted_iota(jnp.int32, (SUBLANE, t), 0)
    b16 = jnp.zeros((k, t), F32)
    a_hi = jnp.zeros((SUBLANE, t), F32)
    for i in range(k):
        b16 = jnp.where(row16 == i, b[i], b16)
    for i in range(SUBLANE):
        a_hi = jnp.where(row8 == i, a[SUBLANE + i], a_hi)
    b8 = b16[:SUBLANE]
    parts = [a[0] + b16]
    masked = 0
    for p in range(1, SUBLANE):
        limit = k // (p + 1)
        parts.append(jnp.where(row8 < limit, a[p] + b8, -jnp.inf))
        masked += SUBLANE - limit
    parts.append(a_hi + b[0])
    return jnp.concatenate(parts, axis=0), masked


_NO_RANK = 1024.0


def _ranked_top(v, k):
    index = lax.broadcasted_iota(jnp.int32, v.shape, 0).astype(F32)
    rank = jnp.full(v.shape, _NO_RANK, F32)
    vals = []
    for t in range(k):
        m = jnp.max(v, axis=0, keepdims=True)
        first = jnp.min(jnp.where(v == m, index, float(v.shape[0])), axis=0, keepdims=True)
        hit = index == first
        rank = jnp.where(hit, float(t + 1), rank)
        vals.append(m)
        v = jnp.where(hit, -jnp.inf, v)
    return rank, vals


def _ranked_row_counts(a, b):
    k = len(a)
    t = a[0].shape[1]
    row = lax.broadcasted_iota(jnp.int32, (k, t), 0).astype(F32)
    b_col = jnp.zeros((k, t), F32)
    for q in range(k):
        b_col = jnp.where(row == float(q), b[q], b_col)
    cand = jnp.concatenate([a[p] + b_col for p in range(k)], axis=0)
    flat = lax.broadcasted_iota(jnp.int32, cand.shape, 0).astype(F32)
    taken = jnp.zeros((k, t), F32)
    z = jnp.zeros((1, t), F32)
    best = None
    for _ in range(k):
        m = jnp.max(cand, axis=0, keepdims=True)
        first = jnp.min(jnp.where(cand == m, flat, float(k * k)), axis=0, keepdims=True)
        taken = taken + jnp.where(row == jnp.floor(first / float(k)), 1.0, 0.0)
        best = m if best is None else best
        z = z + jnp.exp(m - best)
        cand = jnp.where(flat == first, -jnp.inf, cand)
    return taken, z


def _route_kernel(x_ref, g_ref, wqt_ref, keys_ref,
                  xnt_ref, s1_ref, c1_ref, s2_ref, e2_ref, tau_ref, q_s):
    nheads = s1_ref.shape[0]
    nkeys = s1_ref.shape[1]
    xn = _rmsnorm(x_ref[...], g_ref[...])
    xnt = xn.T.astype(BF16)
    xnt_ref[...] = xnt
    q_s[...] = jnp.dot(wqt_ref[...], xnt, preferred_element_type=F32)

    def head(h, carry):
        tops, dup, expd, sc = [], [], [], []
        for c in range(2):
            r0 = pl.multiple_of((h * 2 + c) * nkeys, nkeys)
            qhc = q_s[pl.ds(r0, nkeys), :].astype(BF16)
            s = jnp.dot(keys_ref[h * 2 + c], qhc, preferred_element_type=F32)
            t, removed = _top_values(s, PEER_TOPK + 1)
            tops.append(t)
            dup.append(removed != float(PEER_TOPK + 1))
            expd.append(jnp.exp(s - t[0]))
            sc.append(s)
        cand, masked = _candidate_sums(tops[0][:PEER_TOPK], tops[1][:PEER_TOPK])
        best, removed = _top_values(cand, PEER_TOPK, masked)
        tau = best[PEER_TOPK - 1]
        z = jnp.ones_like(tau)
        for t in range(1, PEER_TOPK):
            z = z + jnp.exp(best[t] - best[0])
        tie = dup[0] | dup[1] | (removed != float(PEER_TOPK))
        tie = tie | (tops[0][PEER_TOPK] + tops[1][0] >= tau) | (tops[0][0] + tops[1][PEER_TOPK] >= tau)
        s1_ref[h] = sc[0]
        c1_ref[h] = expd[0] / z
        s2_ref[h] = sc[1]
        e2_ref[h] = expd[1]
        tau_ref[h] = tau

        @pl.when(jnp.max(jnp.where(tie, 1.0, 0.0)) > 0.0)
        def _():
            rank1, a = _ranked_top(sc[0], PEER_TOPK)
            rank2, b = _ranked_top(sc[1], PEER_TOPK)
            taken, z_ranked = _ranked_row_counts(a, b)
            width = jnp.zeros_like(sc[0])
            for p in range(PEER_TOPK):
                width = jnp.where(rank1 == float(p + 1), taken[p:p + 1, :], width)
            s1_ref[h] = jnp.where(tie, width, sc[0])
            s2_ref[h] = jnp.where(tie, -rank2, sc[1])
            tau_ref[h] = jnp.where(tie, 0.0, tau)
            c1_ref[h] = expd[0] / jnp.where(tie, z_ranked, z)

        return carry

    lax.fori_loop(0, nheads, head, 0)


def _route_call(xall, g, wqt, keys):
    nt, d = xall.shape
    nhc, nkeys, _ = keys.shape
    nheads = nhc // 2
    tm = ROUTE_TILE
    tok = lambda i: (0, 0, i)
    big = jax.ShapeDtypeStruct((nheads, nkeys, nt), F32)
    small = jax.ShapeDtypeStruct((nheads, 1, nt), F32)
    return pl.pallas_call(
        _route_kernel,
        grid=(nt // tm,),
        in_specs=[pl.BlockSpec((tm, d), lambda i: (i, 0)), _full(g.shape), _full(wqt.shape),
                  _full(keys.shape)],
        out_specs=[pl.BlockSpec((d, tm), lambda i: (0, i))]
        + [pl.BlockSpec((nheads, nkeys, tm), tok)] * 4
        + [pl.BlockSpec((nheads, 1, tm), tok)],
        out_shape=[jax.ShapeDtypeStruct((d, nt), BF16), big, big, big, big, small],
        scratch_shapes=[pltpu.VMEM((wqt.shape[0], tm), F32)],
        compiler_params=_cparams(("arbitrary",)),
        name="peer_route",
    )(xall, g, wqt, keys)


def _expert_kernel(x_ref, xnt_ref, s1_ref, c1_ref, s2_ref, e2_ref, tau_ref, u_ref, vt_ref,
                   xo_ref, acc, a_s, w_s, *, jblk):
    c = pl.program_id(1)
    nheads, nkeys, tm = s2_ref.shape
    ec = u_ref.shape[0]
    nrow_i = ec // nkeys
    njb = nkeys // jblk
    slab = nrow_i * jblk
    mm_lanes = 2 * LANE
    i0 = pl.multiple_of(c * nrow_i, SUBLANE)

    @pl.when(c == 0)
    def _():
        acc[...] = jnp.zeros_like(acc)

    def gates(l0, jb):
        lanes = pl.ds(l0, LANE)
        jrows = slice(jb * jblk, (jb + 1) * jblk)
        for first in range(0, nrow_i, GATE_ROWS):
            iis = range(first, first + GATE_ROWS)
            g = {ii: jnp.zeros((jblk, LANE), F32) for ii in iis}
            for h in range(nheads):
                s1 = s1_ref[h, pl.ds(i0, SUBLANE), lanes]
                c1 = c1_ref[h, pl.ds(i0, SUBLANE), lanes]
                s2 = s2_ref[h, jrows, lanes]
                e2 = e2_ref[h, jrows, lanes]
                tau = tau_ref[h, :, lanes]
                for ii in iis:
                    sel = (s1[ii:ii + 1, :] + s2) >= tau
                    g[ii] = g[ii] + jnp.where(sel, e2 * c1[ii:ii + 1, :], 0.0)
            for ii in iis:
                rows = slice(jb * slab + ii * jblk, jb * slab + (ii + 1) * jblk)
                w_s[rows, lanes] = (g[ii] * jax.nn.gelu(a_s[rows, lanes])).astype(BF16)

    def piece(l, carry):
        m0 = pl.multiple_of(l * mm_lanes, mm_lanes)
        lanes = pl.ds(m0, mm_lanes)

        for jb in range(njb):
            rows = slice(jb * slab, (jb + 1) * slab)
            a_s[rows, lanes] = jnp.dot(u_ref[rows, :], xnt_ref[:, lanes],
                                       preferred_element_type=F32)
            for sub in range(mm_lanes // LANE):
                gates(pl.multiple_of(m0 + sub * LANE, LANE), jb)
            acc[:, lanes] += jnp.dot(vt_ref[:, rows], w_s[rows, lanes],
                                     preferred_element_type=F32)
        return carry

    lax.fori_loop(0, tm // mm_lanes, piece, 0)

    @pl.when(c == pl.num_programs(1) - 1)
    def _():
        xo_ref[...] = x_ref[...] + acc[...].T


def _expert_call(xall, xnt, s1, c1, s2, e2, tau, u, vt):
    nt, d = xall.shape
    nheads, nkeys, _ = s1.shape
    nexp = u.shape[0]
    tm, ec = TOK_TILE, EXPERT_CHUNK
    tok = lambda i, c: (0, 0, i)
    return pl.pallas_call(
        functools.partial(_expert_kernel, jblk=GATE_BLOCK),
        grid=(nt // tm, nexp // ec),
        in_specs=[pl.BlockSpec((tm, d), lambda i, c: (i, 0)),
                  pl.BlockSpec((d, tm), lambda i, c: (0, i))]
        + [pl.BlockSpec((nheads, nkeys, tm), tok)] * 4
        + [pl.BlockSpec((nheads, 1, tm), tok),
           pl.BlockSpec((ec, d), lambda i, c: (c, 0)),
           pl.BlockSpec((d, ec), lambda i, c: (0, c))],
        out_specs=pl.BlockSpec((tm, d), lambda i, c: (i, 0)),
        out_shape=jax.ShapeDtypeStruct((nt, d), F32),
        scratch_shapes=[pltpu.VMEM((d, tm), F32), pltpu.VMEM((ec, tm), F32),
                        pltpu.VMEM((ec, tm), BF16)],
        input_output_aliases={0: 0},
        compiler_params=_cparams(("arbitrary", "arbitrary")),
        name="peer_experts",
    )(xall, xnt, s1, c1, s2, e2, tau, u, vt)


def _slab_order(table, nkeys):
    nexp, d = table.shape
    nrow_i = EXPERT_CHUNK // nkeys
    t = table.reshape(nexp // EXPERT_CHUNK, nrow_i, nkeys // GATE_BLOCK, GATE_BLOCK, d)
    return t.transpose(0, 2, 1, 3, 4).reshape(nexp, d)


def _peer(xall, g, wqt, keys, u, vt):
    xnt, s1, c1, s2, e2, tau = _route_call(xall, g, wqt, keys)
    return _expert_call(xall, xnt, s1, c1, s2, e2, tau, u, vt)


def _exact_group_sum(sq, ones_bd):
    hi = sq.astype(BF16)
    r1 = sq - hi.astype(F32)
    mid = r1.astype(BF16)
    lo = (r1 - mid.astype(F32)).astype(BF16)
    dot = lambda t: jnp.dot(t, ones_bd, preferred_element_type=F32)
    return dot(hi) + dot(mid) + dot(lo)


def _head_norm_rope(y, g2, cos, sin_signed, ones_bd, first_half, head_dim):
    out = []
    for n in range(y.shape[1] // LANE):
        blk = y[:, n * LANE:(n + 1) * LANE]
        ms = _exact_group_sum(blk * blk, ones_bd) / head_dim
        z = blk * lax.rsqrt(ms + EPS) * g2
        partner = jnp.where(first_half, pltpu.roll(z, LANE - head_dim // 2, 1),
                            pltpu.roll(z, head_dim // 2, 1))
        out.append(z * cos + partner * sin_signed)
    return jnp.concatenate(out, axis=1)


def _qkv_kernel(x_ref, cos_ref, sin_ref, gkv_ref, gq_ref, wk_ref, wv_ref, wq_ref, gk2_ref,
                gq2_ref, k_ref, v_ref, kb_ref, vt_ref, qt_ref, qb_ref, *, head_dim):
    x = x_ref[...]
    cos = cos_ref[...]
    sin_signed = sin_ref[...]
    lane = lax.broadcasted_iota(jnp.int32, (1, LANE), 1)
    first_half = (lane % head_dim) < head_dim // 2
    r = lax.broadcasted_iota(jnp.int32, (LANE, LANE), 0) // head_dim
    cidx = lax.broadcasted_iota(jnp.int32, (LANE, LANE), 1) // head_dim
    ones_bd = jnp.where(r == cidx, 1.0, 0.0).astype(BF16)

    hk = _rmsnorm(x, gkv_ref[...]).astype(BF16)
    k = _head_norm_rope(jnp.dot(hk, wk_ref[...], preferred_element_type=F32), gk2_ref[...],
                        cos, sin_signed, ones_bd, first_half, head_dim)
    v = jnp.dot(hk, wv_ref[...], preferred_element_type=F32)
    hq = _rmsnorm(x, gq_ref[...]).astype(BF16)
    q = _head_norm_rope(jnp.dot(hq, wq_ref[...], preferred_element_type=F32), gq2_ref[...],
                        cos, sin_signed, ones_bd, first_half, head_dim)
    k_ref[...] = k
    v_ref[...] = v
    kb_ref[...] = k.astype(BF16)
    vt_ref[...] = v.T.astype(BF16)
    qs = q * head_dim ** -0.5
    qt_ref[...] = qs.T.astype(BF16)
    qb_ref[...] = qs.astype(BF16)


def _qkv_call(xall, cos, sin_signed, gkv, gq, wk, wv, wq, gk2, gq2, head_dim):
    nt, d = xall.shape
    tm = TOK_TILE
    row = lambda i: (i, 0)
    ws = [gkv, gq, wk, wv, wq, gk2, gq2]
    return pl.pallas_call(
        functools.partial(_qkv_kernel, head_dim=head_dim),
        grid=(nt // tm,),
        in_specs=[pl.BlockSpec((tm, d), row), pl.BlockSpec((tm, LANE), row),
                  pl.BlockSpec((tm, LANE), row)] + [_full(a.shape) for a in ws],
        out_specs=[pl.BlockSpec((tm, d), row)] * 3 + [pl.BlockSpec((d, tm), lambda i: (0, i))] * 2
        + [pl.BlockSpec((tm, d), row)],
        out_shape=[jax.ShapeDtypeStruct((nt, d), F32)] * 2 + [jax.ShapeDtypeStruct((nt, d), BF16)]
        + [jax.ShapeDtypeStruct((d, nt), BF16)] * 2 + [jax.ShapeDtypeStruct((nt, d), BF16)],
        compiler_params=_cparams(("arbitrary",)),
        name="qkv_proj",
    )(xall, cos, sin_signed, *ws)


def _prompt_attn_kernel(*refs, head_dim, nstream):
    qt_refs = refs[:nstream]
    k_refs = refs[nstream:2 * nstream]
    vt_refs = refs[2 * nstream:3 * nstream]
    a0_ref, a1_ref = refs[3 * nstream:]
    qi = pl.program_id(1)
    tq = qt_refs[0].shape[1]
    tk = tq
    feat = lax.broadcasted_iota(jnp.int32, (LANE, 1), 0)
    qs = []
    for qt_ref in qt_refs:
        qt = qt_ref[...]
        zero = jnp.zeros_like(qt)
        qs.append(jnp.concatenate([jnp.where(feat < head_dim, qt, zero),
                                   jnp.where(feat >= head_dim, qt, zero)], axis=1))

    def update(carry, j, masked):
        k0 = pl.multiple_of(j * tk, tk)
        s = [jnp.dot(k_refs[b][pl.ds(k0, tk), :], qs[b], preferred_element_type=F32)
             for b in range(nstream)]
        out = []
        for b in range(nstream):
            m, l, acc = carry[b]
            sb = s[b]
            if masked:
                kpos = lax.broadcasted_iota(jnp.int32, (tk, 2 * tq), 0)
                qpos = lax.broadcasted_iota(jnp.int32, (tk, 2 * tq), 1) % tq
                sb = jnp.where(kpos <= qpos, sb, NEG)
            m_new = jnp.maximum(m, jnp.max(sb, axis=0, keepdims=True))
            corr = jnp.exp(m - m_new)
            p = jnp.exp(sb - m_new)
            l = l * corr + jnp.sum(p, axis=0, keepdims=True)
            acc = acc * corr + jnp.dot(vt_refs[b][:, pl.ds(k0, tk)], p.astype(BF16),
                                       preferred_element_type=F32)
            out.append((m_new, l, acc))
        return tuple(out)

    init = tuple((jnp.full((1, 2 * tq), NEG, F32), jnp.zeros((1, 2 * tq), F32),
                  jnp.zeros((LANE, 2 * tq), F32)) for _ in range(nstream))
    carry = lax.fori_loop(0, qi, lambda j, cr: update(cr, j, False), init)
    carry = update(carry, qi, True)
    for b, (m, l, acc) in enumerate(carry):
        out = acc / l
        a0_ref[b] = out[:, :tq]
        a1_ref[b] = out[:, tq:]


def _prompt_attn_call(qt, kb, vt, nbatch, tp, head_dim):
    d, nt = qt.shape
    nheads = d // LANE
    tq = ATT_TILE
    nq = tp // tq
    bind = lambda f, b: functools.partial(f, b=b)
    streams = range(nbatch)
    return pl.pallas_call(
        functools.partial(_prompt_attn_kernel, head_dim=head_dim, nstream=nbatch),
        grid=(nheads, nq),
        in_specs=[pl.BlockSpec((LANE, tq), bind(lambda h, i, b: (h, b * nq + i), b))
                  for b in streams]
        + [pl.BlockSpec((tp, LANE), bind(lambda h, i, b: (b, h), b)) for b in streams]
        + [pl.BlockSpec((LANE, tp), bind(lambda h, i, b: (h, b), b)) for b in streams],
        out_specs=[pl.BlockSpec((nbatch, LANE, tq), lambda h, i: (0, h, i))] * 2,
        out_shape=[jax.ShapeDtypeStruct((nbatch, d, tp), F32)] * 2,
        compiler_params=_cparams(("arbitrary", "arbitrary")),
        name="prompt_attention",
    )(*([qt] * nbatch), *([kb] * nbatch), *([vt] * nbatch))


def _decode_attn_kernel(pt_ref, q_ref, kn_ref, vn_ref, *refs, head_dim, npg):
    del pt_ref
    kp = refs[:npg]
    vp = refs[npg:2 * npg]
    o_ref, qblk, expand, hmask, m_s, l_s, acc = refs[2 * npg:]
    p = pl.program_id(1)
    nrow, d = qblk.shape
    nheads = d // LANE
    ntok = nrow // (2 * nheads)
    psz = expand.shape[0]

    @pl.when(p == 0)
    def _():
        q = q_ref[0].astype(F32)
        row = lax.broadcasted_iota(jnp.int32, (nrow, d), 0)
        col = lax.broadcasted_iota(jnp.int32, (nrow, d), 1)
        rep = jnp.concatenate(
            [jnp.broadcast_to(q[t:t + 1, :], (nheads, d)) for t in range(ntok)] * 2, axis=0)
        mine = (col // LANE == row % nheads) & ((col % LANE) // head_dim == row // (ntok * nheads))
        qblk[...] = jnp.where(mine, rep, 0.0).astype(BF16)
        pos = lax.broadcasted_iota(jnp.int32, expand.shape, 0)
        ecol = lax.broadcasted_iota(jnp.int32, expand.shape, 1)
        expand[...] = jnp.where(ecol // nheads == pos, 1.0, 0.0).astype(BF16)
        hrow = lax.broadcasted_iota(jnp.int32, hmask.shape, 0)
        hcol = lax.broadcasted_iota(jnp.int32, hmask.shape, 1)
        hmask[...] = jnp.where(hcol % nheads == hrow % nheads, 1.0, 0.0)
        m_s[...] = jnp.full_like(m_s, NEG)
        l_s[...] = jnp.zeros_like(l_s)
        acc[...] = jnp.zeros_like(acc)

    def update(s, vs):
        m = m_s[...]
        m_new = jnp.maximum(m, jnp.max(s, axis=-1, keepdims=True))
        corr = jnp.exp(m - m_new)
        pr = jnp.exp(s - m_new)
        l_s[...] = l_s[...] * corr + jnp.sum(pr, axis=-1, keepdims=True)
        m_s[...] = m_new
        pv = acc[...] * corr
        for i, v in enumerate(vs):
            spread = jnp.dot(pr[:, i * psz:(i + 1) * psz].astype(BF16), expand[...],
                             preferred_element_type=F32) * hmask[...]
            pv = pv + jnp.dot(spread.astype(BF16), v, preferred_element_type=F32)
        acc[...] = pv

    qb = qblk[...]
    s = jnp.concatenate([jnp.dot(qb, kp[i][0].astype(BF16), preferred_element_type=F32)
                         for i in range(npg)], axis=1)
    update(s, [vp[i][0].astype(BF16) for i in range(npg)])

    @pl.when(p == pl.num_programs(1) - 1)
    def _():
        s = jnp.dot(qblk[...], kn_ref[0].astype(BF16), preferred_element_type=F32)
        trow = (lax.broadcasted_iota(jnp.int32, s.shape, 0) // nheads) % ntok
        tcol = lax.broadcasted_iota(jnp.int32, s.shape, 1)
        update(jnp.where(tcol <= trow, s, NEG), [vn_ref[0].astype(BF16)])
        o_ref[0] = acc[...] / l_s[...]


def _decode_attn_call(page_table, q, knt, vn, cache_kt, cache_v, head_dim):
    nb, ntok, d = q.shape
    npages = page_table.shape[1]
    npg = PAGES_PER_STEP
    psz = cache_kt.shape[2]
    nheads = d // LANE
    nrow = 2 * ntok * nheads
    bmap = lambda b, p, pt: (b, 0, 0)
    pmaps = [functools.partial(lambda b, p, pt, i: (pt[b, p * npg + i], 0, 0), i=i)
             for i in range(npg)]
    grid_spec = pltpu.PrefetchScalarGridSpec(
        num_scalar_prefetch=1,
        grid=(nb, npages // npg),
        in_specs=[pl.BlockSpec((1, ntok, d), bmap), pl.BlockSpec((1, d, psz), bmap),
                  pl.BlockSpec((1, psz * nheads, LANE), bmap)]
        + [pl.BlockSpec((1, d, psz), pm) for pm in pmaps]
        + [pl.BlockSpec((1, psz * nheads, LANE), pm) for pm in pmaps],
        out_specs=pl.BlockSpec((1, nrow, LANE), bmap),
        scratch_shapes=[pltpu.VMEM((nrow, d), BF16),
                        pltpu.VMEM((psz, psz * nheads), BF16),
                        pltpu.VMEM((nrow, psz * nheads), F32),
                        pltpu.VMEM((nrow, 1), F32),
                        pltpu.VMEM((nrow, 1), F32),
                        pltpu.VMEM((nrow, LANE), F32)])
    return pl.pallas_call(
        functools.partial(_decode_attn_kernel, head_dim=head_dim, npg=npg),
        grid_spec=grid_spec,
        out_shape=jax.ShapeDtypeStruct((nb, nrow, LANE), F32),
        compiler_params=_cparams(("arbitrary", "arbitrary")),
        name="decode_attention",
    )(page_table, q, knt, vn, *([cache_kt] * npg), *([cache_v] * npg))


def _combine_kernel(x_ref, a0_ref, a1_ref, lp_ref, gs_ref, wot_ref, xo_ref, *, lambda_init):
    lp = lp_ref[...]
    lam = (jnp.exp(jnp.sum(lp[0:1] * lp[1:2], axis=-1, keepdims=True))
           - jnp.exp(jnp.sum(lp[2:3] * lp[3:4], axis=-1, keepdims=True)) + lambda_init)
    o = a0_ref[...] - lam * a1_ref[...]
    parts = []
    for h in range(o.shape[0] // LANE):
        oh = o[h * LANE:(h + 1) * LANE, :]
        ms = jnp.mean(oh * oh, axis=0, keepdims=True)
        parts.append(oh * lax.rsqrt(ms + EPS) * gs_ref[...] * (1.0 - lambda_init))
    ob = jnp.concatenate(parts, axis=0).astype(BF16)
    yt = jnp.dot(wot_ref[...], ob, preferred_element_type=F32)
    xo_ref[...] = x_ref[...] + yt.T


def _combine_call(xall, a0t, a1t, lp, gs, wot, lambda_init):
    nt, d = xall.shape
    tm = TOK_TILE
    row = lambda i: (i, 0)
    col = lambda i: (0, i)
    return pl.pallas_call(
        functools.partial(_combine_kernel, lambda_init=lambda_init),
        grid=(nt // tm,),
        in_specs=[pl.BlockSpec((tm, d), row), pl.BlockSpec((d, tm), col),
                  pl.BlockSpec((d, tm), col), _full(lp.shape), _full(gs.shape),
                  _full(wot.shape)],
        out_specs=pl.BlockSpec((tm, d), row),
        out_shape=jax.ShapeDtypeStruct((nt, d), F32),
        input_output_aliases={0: 0},
        compiler_params=_cparams(("arbitrary",)),
        name="diff_combine_out_proj",
    )(xall, a0t, a1t, lp, gs, wot)


def _round_up(n, m):
    return -(-n // m) * m


def kernel(x_prompt, x_sample, state_lru_h, state_conv, cache_k, cache_v, page_table, meta_tokens, g_mix, g_ffn, lru_w_gate, lru_w_in, lru_conv_w, lru_conv_b, lru_w_a, lru_b_a, lru_w_x, lru_b_x, lru_lambda, lru_w_out, kv_norm, w_k, w_v, k_norm, w_q, q_norm, lam_params, sub_norm, w_o, peer_w_q, peer_keys, peer_u, peer_v):
    nbatch, seq, d = x_prompt.shape
    nb_dec, dec_seq, _ = x_sample.shape
    depth = g_mix.shape[0]
    assert depth == 2 and lru_w_gate.shape[0] == 1 and w_q.shape[0] == 1
    head_dim = k_norm.shape[0]
    nheads = d // (2 * head_dim)
    assert 2 * head_dim == LANE
    past_len = page_table.shape[1] * cache_k.shape[1]
    t_real = seq + N_META
    tp = _round_up(t_real, math.lcm(LRU_TILE, ATT_TILE))
    ns = nb_dec * dec_seq
    row_s = nbatch * tp
    assert row_s % ns == 0
    nt = _round_up(row_s + ns, math.lcm(TOK_TILE, ROUTE_TILE))
    row2 = lambda a: a.reshape(1, -1)

    pieces = []
    for b in range(nbatch):
        pieces += [meta_tokens.astype(F32), x_prompt[b], jnp.zeros((tp - t_real, d), F32)]
    pieces += [x_sample.transpose(1, 0, 2).reshape(ns, d), jnp.zeros((nt - row_s - ns, d), F32)]
    xall = jnp.concatenate(pieces, axis=0)

    lw = [row2(g_mix[0]), lru_w_gate[0].astype(BF16), lru_w_in[0].astype(BF16), lru_conv_w[0],
          row2(lru_conv_b[0]), lru_w_a[0].astype(BF16), row2(lru_b_a[0]),
          lru_w_x[0].astype(BF16), row2(lru_b_x[0]), row2(lru_lambda[0]),
          lru_w_out[0].astype(BF16)]
    xall, h_p, buf_p = _lru_call(
        xall, jnp.zeros((nbatch, 1, d), F32), jnp.zeros((nbatch, CONV_W - 1, d), F32), lw,
        nbatch=nbatch, nseq=1, rows=LRU_TILE, ntile=tp // LRU_TILE, row0=0, reset_first=True,
        last_t=t_real - 1)
    buf0_s = state_conv[0].transpose(1, 0, 2).reshape(1, (CONV_W - 1) * nb_dec, d)
    xall, h_s, buf_s = _lru_call(
        xall, state_lru_h[0][None], buf0_s, lw,
        nbatch=1, nseq=nb_dec, rows=ns, ntile=1, row0=row_s, reset_first=False,
        last_t=dec_seq - 1)

    def peer_layer(x, layer):
        keys = peer_keys[layer].reshape(-1, peer_keys.shape[3], peer_keys.shape[4]).astype(BF16)
        nkeys = peer_keys.shape[3]
        return _peer(x, row2(g_ffn[layer]), peer_w_q[layer].T.astype(BF16), keys,
                     _slab_order(peer_u[layer].astype(BF16), nkeys),
                     _slab_order(peer_v[layer].astype(BF16), nkeys).T)

    xall = peer_layer(xall, 0)

    half = head_dim // 2
    inv = ROPE_THETA ** (-jnp.arange(half, dtype=F32) / half)
    rows = jnp.arange(nt)
    pos = jnp.where(rows < row_s, rows % tp,
                    jnp.where(rows < row_s + ns, past_len + (rows - row_s) // nb_dec, 0))
    ang = pos.astype(F32)[:, None] * inv[None, :]
    cos = jnp.tile(jnp.cos(ang), (1, LANE // half))
    sin = jnp.sin(ang)
    sin_signed = jnp.tile(jnp.concatenate([-sin, sin], axis=1), (1, LANE // head_dim))
    tile2 = lambda g: jnp.tile(g, LANE // head_dim).reshape(1, LANE)
    k_all, v_all, kb, vt, qt, qb = _qkv_call(
        xall, cos, sin_signed, row2(kv_norm), row2(g_mix[1]), w_k.astype(BF16),
        w_v.astype(BF16), w_q[0].astype(BF16), tile2(k_norm), tile2(q_norm[0]), head_dim)

    a0p, a1p = _prompt_attn_call(qt, kb, vt, nbatch, tp, head_dim)

    to_seq = lambda a: a[row_s:row_s + ns].reshape(dec_seq, nb_dec, d).transpose(1, 0, 2)
    npool, psz = cache_k.shape[:2]
    pad_tok = ((0, 0), (0, psz - dec_seq), (0, 0))
    knt = jnp.pad(to_seq(k_all), pad_tok).transpose(0, 2, 1)
    vn = jnp.pad(to_seq(v_all), pad_tok).reshape(nb_dec, psz * nheads, LANE)
    o_s = _decode_attn_call(
        page_table, to_seq(qb), knt, vn,
        cache_k.reshape(npool, psz, d).transpose(0, 2, 1),
        cache_v.reshape(npool, psz * nheads, LANE), head_dim)
    o_s = o_s.reshape(nb_dec, 2, dec_seq, d)
    from_seq_t = lambda a: a.transpose(2, 1, 0).reshape(d, ns)
    tail = jnp.zeros((d, nt - row_s - ns), F32)
    columns = lambda ap, a_s: jnp.concatenate(
        [ap[b] for b in range(nbatch)] + [from_seq_t(a_s), tail], axis=1)
    a0t = columns(a0p, o_s[:, 0])
    a1t = columns(a1p, o_s[:, 1])

    lambda_init = 0.8 - 0.6 * math.exp(-0.3 * 1)
    xall = _combine_call(xall, a0t, a1t, lam_params[0], sub_norm[0].reshape(-1, 1),
                         w_o[0].T.astype(BF16), lambda_init)
    xall = peer_layer(xall, 1)

    prompt = lambda a, t0: jnp.stack([a[b * tp + t0:b * tp + t_real] for b in range(nbatch)])
    y_prompt = prompt(xall, N_META)
    y_sample = to_seq(xall)
    k_p = prompt(k_all, 0).reshape(nbatch, t_real, nheads, 2, head_dim)
    v_p = prompt(v_all, 0).reshape(nbatch, t_real, nheads, 2 * head_dim)
    k_s = to_seq(k_all).reshape(nb_dec, dec_seq, nheads, 2, head_dim)
    v_s = to_seq(v_all).reshape(nb_dec, dec_seq, nheads, 2 * head_dim)
    conv_s = buf_s.reshape(CONV_W - 1, nb_dec, d).transpose(1, 0, 2)[None]
    return (y_prompt, y_sample, h_p.reshape(1, nbatch, d), buf_p[None], k_p, v_p,
            h_s, conv_s, k_s, v_s)
```

```python
import functools
import math

import jax
import jax.numpy as jnp
from jax import lax
from jax.experimental import pallas as pl
from jax.experimental.pallas import tpu as pltpu

F32 = jnp.float32
BF16 = jnp.bfloat16

EPS = 1e-6
NEG = -1e30
LRU_C = 8.0
ROPE_THETA = 10000.0
CONV_W = 4
N_META = 16
PEER_TOPK = 16

LANE = 128
SUBLANE = 8
VMEM_LIMIT = 56 * 1024 * 1024

LRU_TILE = 640
ATT_TILE = 640
TOK_TILE = 512
ROUTE_TILE = 256
EXPERT_CHUNK = 1024
GATE_BLOCK = 32
GATE_ROWS = 4
PAGES_PER_STEP = 8


def _cparams(sem):
    return pltpu.CompilerParams(dimension_semantics=sem, vmem_limit_bytes=VMEM_LIMIT)


def _rmsnorm(x, g):
    ms = jnp.mean(x * x, axis=-1, keepdims=True)
    return x * lax.rsqrt(ms + EPS) * g


def _full(shape):
    n = len(shape)
    return pl.BlockSpec(shape, lambda *_: (0,) * n)


def _lru_kernel(x_ref, h0_ref, buf0_ref, g_ref, wg_ref, wi_ref, cw_ref, cb_ref, wa_ref, ba_ref,
                wx_ref, bx_ref, lam_ref, wo_ref,
                xo_ref, ht_ref, buft_ref,
                ubuf, a_s, u_s, hs_s, h_s, *, nseq, rows, reset_first, last_t):
    ti = pl.program_id(1)
    d = x_ref.shape[-1]
    tail = (CONV_W - 1) * nseq
    off = -(-tail // SUBLANE) * SUBLANE

    @pl.when(ti == 0)
    def _():
        h_s[...] = h0_ref[0]
        ubuf[off - tail:off, :] = buf0_ref[0]

    x = x_ref[...]
    xb = _rmsnorm(x, g_ref[...]).astype(BF16)
    gate = jax.nn.gelu(jnp.dot(xb, wg_ref[...], preferred_element_type=F32))
    ubuf[off:off + rows, :] = jnp.dot(xb, wi_ref[...], preferred_element_type=F32)

    y = cb_ref[...]
    for k in range(CONV_W):
        s = off - (CONV_W - 1 - k) * nseq
        y = y + ubuf[s:s + rows, :] * cw_ref[k:k + 1, :]

    steps = rows // nseq
    lt_tile, lt_loc = divmod(last_t, steps)

    @pl.when(ti == lt_tile)
    def _():
        s = off + (lt_loc - (CONV_W - 2)) * nseq
        buft_ref[0] = ubuf[s:s + tail, :]

    ubuf[off - tail:off, :] = ubuf[off + rows - tail:off + rows, :]

    lam = -lam_ref[...]
    softplus = jnp.maximum(lam, 0.0) + jnp.log1p(jnp.exp(-jnp.abs(lam)))
    nblk = wa_ref.shape[0]
    bw = d // nblk
    if reset_first:
        row = lax.broadcasted_iota(jnp.int32, (rows, bw), 0)
        is_t0 = row < jnp.where(ti == 0, nseq, 0)
    for n in range(nblk):
        sl = slice(n * bw, (n + 1) * bw)
        yn = y[:, sl]
        ybn = yn.astype(BF16)
        r = jax.nn.sigmoid(jnp.dot(ybn, wa_ref[n], preferred_element_type=F32) + ba_ref[:, sl])
        i = jax.nn.sigmoid(jnp.dot(ybn, wx_ref[n], preferred_element_type=F32) + bx_ref[:, sl])
        log_a = -LRU_C * r * softplus[:, sl]
        a = jnp.exp(log_a)
        mult = jnp.sqrt(-jnp.tanh(log_a) * (a * a + 1.0))
        if reset_first:
            mult = jnp.where(is_t0, 1.0, mult)
        a_s[:, sl] = a
        u_s[:, sl] = mult * i * yn

    if nseq == 1:
        row8 = lax.broadcasted_iota(jnp.int32, (SUBLANE, d), 0)

        def group(gi, h):
            base = pl.multiple_of(gi * SUBLANE, SUBLANE)
            a8 = a_s[pl.ds(base, SUBLANE), :]
            u8 = u_s[pl.ds(base, SUBLANE), :]
            hs8 = jnp.zeros_like(a8)
            for r_ in range(SUBLANE):
                h = a8[r_:r_ + 1, :] * h + u8[r_:r_ + 1, :]
                hs8 = jnp.where(row8 == r_, h, hs8)
            hs_s[pl.ds(base, SUBLANE), :] = hs8
            return h
        h = lax.fori_loop(0, rows // SUBLANE, group, h_s[...])
    else:
        h = h_s[...]
        for t in range(steps):
            h = a_s[t * nseq:(t + 1) * nseq, :] * h + u_s[t * nseq:(t + 1) * nseq, :]
            hs_s[t * nseq:(t + 1) * nseq, :] = h
    h_s[...] = h

    @pl.when(ti == lt_tile)
    def _():
        ht_ref[0] = hs_s[lt_loc * nseq:(lt_loc + 1) * nseq, :]

    go = (gate * hs_s[...]).astype(BF16)
    xo_ref[...] = x + jnp.dot(go, wo_ref[...], preferred_element_type=F32)


def _lru_call(xall, h0, buf0, w, *, nbatch, nseq, rows, ntile, row0, reset_first, last_t):
    nt, d = xall.shape
    tail = (CONV_W - 1) * nseq
    off = -(-tail // SUBLANE) * SUBLANE
    blk0 = row0 // rows
    xmap = lambda b, t: (blk0 + b * ntile + t, 0)
    smap = lambda b, t: (b, 0, 0)
    kern = functools.partial(_lru_kernel, nseq=nseq, rows=rows, reset_first=reset_first,
                             last_t=last_t)
    return pl.pallas_call(
        kern,
        grid=(nbatch, ntile),
        in_specs=[pl.BlockSpec((rows, d), xmap),
                  pl.BlockSpec((1, nseq, d), smap),
                  pl.BlockSpec((1, tail, d), smap)] + [_full(a.shape) for a in w],
        out_specs=[pl.BlockSpec((rows, d), xmap),
                   pl.BlockSpec((1, nseq, d), smap),
                   pl.BlockSpec((1, tail, d), smap)],
        out_shape=[jax.ShapeDtypeStruct((nt, d), F32),
                   jax.ShapeDtypeStruct((nbatch, nseq, d), F32),
                   jax.ShapeDtypeStruct((nbatch, tail, d), F32)],
        scratch_shapes=[pltpu.VMEM((off + rows, d), F32),
                        pltpu.VMEM((rows, d), F32),
                        pltpu.VMEM((rows, d), F32),
                        pltpu.VMEM((rows, d), F32),
                        pltpu.VMEM((nseq, d), F32)],
        input_output_aliases={0: 0},
        compiler_params=_cparams(("arbitrary", "arbitrary")),
        name="rg_lru_block",
    )(xall, h0, buf0, *w)


def _top_values(v, n, masked=0):
    tops = []
    for _ in range(n):
        m = jnp.max(v, axis=0, keepdims=True)
        tops.append(m)
        v = jnp.where(v == m, -jnp.inf, v)
    removed = jnp.sum(jnp.where(v == -jnp.inf, 1.0, 0.0), axis=0, keepdims=True) - float(masked)
    return tops, removed


def _candidate_sums(a, b):
    k = len(a)
    assert k == 2 * SUBLANE
    t = a[0].shape[1]
    row16 = lax.broadcasted_iota(jnp.int32, (k, t), 0)
    row8 = lax.broadcasted_iota(jnp.int32, (SUBLANE, t), 0)
    b16 = jnp.zeros((k, t), F32)
    a_hi = jnp.zeros((SUBLANE, t), F32)
    for i in range(k):
        b16 = jnp.where(row16 == i, b[i], b16)
    for i in range(SUBLANE):
        a_hi = jnp.where(row8 == i, a[SUBLANE + i], a_hi)
    b8 = b16[:SUBLANE]
    parts = [a[0] + b16]
    masked = 0
    for p in range(1, SUBLANE):
        limit = k // (p + 1)
        parts.append(jnp.where(row8 < limit, a[p] + b8, -jnp.inf))
        masked += SUBLANE - limit
    parts.append(a_hi + b[0])
    return jnp.concatenate(parts, axis=0), masked


_NO_RANK = 1024.0


def _ranked_top(v, k):
    index = lax.broadcasted_iota(jnp.int32, v.shape, 0).astype(F32)
    rank = jnp.full(v.shape, _NO_RANK, F32)
    vals = []
    for t in range(k):
        m = jnp.max(v, axis=0, keepdims=True)
        first = jnp.min(jnp.where(v == m, index, float(v.shape[0])), axis=0, keepdims=True)
        hit = index == first
        rank = jnp.where(hit, float(t + 1), rank)
        vals.append(m)
        v = jnp.where(hit, -jnp.inf, v)
    return rank, vals


def _ranked_row_counts(a, b):
    k = len(a)
    t = a[0].shape[1]
    row = lax.broadcasted_iota(jnp.int32, (k, t), 0).astype(F32)
    b_col = jnp.zeros((k, t), F32)
    for q in range(k):
        b_col = jnp.where(row == float(q), b[q], b_col)
    cand = jnp.concatenate([a[p] + b_col for p in range(k)], axis=0)
    flat = lax.broadcasted_iota(jnp.int32, cand.shape, 0).astype(F32)
    taken = jnp.zeros((k, t), F32)
    z = jnp.zeros((1, t), F32)
    best = None
    for _ in range(k):
        m = jnp.max(cand, axis=0, keepdims=True)
        first = jnp.min(jnp.where(cand == m, flat, float(k * k)), axis=0, keepdims=True)
        taken = taken + jnp.where(row == jnp.floor(first / float(k)), 1.0, 0.0)
        best = m if best is None else best
        z = z + jnp.exp(m - best)
        cand = jnp.where(flat == first, -jnp.inf, cand)
    return taken, z


def _route_kernel(x_ref, g_ref, wqt_ref, keys_ref,
                  xnt_ref, th_ref, c1_ref, s2_ref, e2_ref, q_s):
    nheads = th_ref.shape[0]
    nkeys = th_ref.shape[1]
    xn = _rmsnorm(x_ref[...], g_ref[...])
    xnt = xn.T.astype(BF16)
    xnt_ref[...] = xnt
    q_s[...] = jnp.dot(wqt_ref[...], xnt, preferred_element_type=F32)

    def head(h, carry):
        tops, dup, expd, sc = [], [], [], []
        for c in range(2):
            r0 = pl.multiple_of((h * 2 + c) * nkeys, nkeys)
            qhc = q_s[pl.ds(r0, nkeys), :].astype(BF16)
            s = jnp.dot(keys_ref[h * 2 + c], qhc, preferred_element_type=F32)
            t, removed = _top_values(s, PEER_TOPK + 1)
            tops.append(t)
            dup.append(removed != float(PEER_TOPK + 1))
            expd.append(jnp.exp(s - t[0]))
            sc.append(s)
        cand, masked = _candidate_sums(tops[0][:PEER_TOPK], tops[1][:PEER_TOPK])
        best, removed = _top_values(cand, PEER_TOPK, masked)
        tau = best[PEER_TOPK - 1]
        z = jnp.ones_like(tau)
        for t in range(1, PEER_TOPK):
            z = z + jnp.exp(best[t] - best[0])
        tie = dup[0] | dup[1] | (removed != float(PEER_TOPK))
        row16 = lax.broadcasted_iota(jnp.int32, (PEER_TOPK, tau.shape[1]), 0)
        b16 = jnp.zeros((PEER_TOPK, tau.shape[1]), F32)
        for q in range(PEER_TOPK):
            b16 = jnp.where(row16 == q, tops[1][q], b16)
        th = jnp.full(sc[0].shape, jnp.inf, F32)
        for p in range(PEER_TOPK):
            a_p = tops[0][p]
            theta = jnp.min(jnp.where(a_p + b16 >= tau, b16, jnp.inf), axis=0, keepdims=True)
            th = jnp.where(sc[0] == a_p, theta, th)
        th_ref[h] = th
        c1_ref[h] = expd[0] / z
        s2_ref[h] = sc[1]
        e2_ref[h] = expd[1]

        @pl.when(jnp.max(jnp.where(tie, 1.0, 0.0)) > 0.0)
        def _():
            rank1, a = _ranked_top(sc[0], PEER_TOPK)
            rank2, b = _ranked_top(sc[1], PEER_TOPK)
            taken, z_ranked = _ranked_row_counts(a, b)
            width = jnp.zeros_like(sc[0])
            for p in range(PEER_TOPK):
                width = jnp.where(rank1 == float(p + 1), taken[p:p + 1, :], width)
            th_ref[h] = jnp.where(tie, -width, th)
            s2_ref[h] = jnp.where(tie, -rank2, sc[1])
            c1_ref[h] = expd[0] / jnp.where(tie, z_ranked, z)

        return carry

    lax.fori_loop(0, nheads, head, 0)


def _route_call(xall, g, wqt, keys):
    nt, d = xall.shape
    nhc, nkeys, _ = keys.shape
    nheads = nhc // 2
    tm = ROUTE_TILE
    tok = lambda i: (0, 0, i)
    big = jax.ShapeDtypeStruct((nheads, nkeys, nt), F32)
    return pl.pallas_call(
        _route_kernel,
        grid=(nt // tm,),
        in_specs=[pl.BlockSpec((tm, d), lambda i: (i, 0)), _full(g.shape), _full(wqt.shape),
                  _full(keys.shape)],
        out_specs=[pl.BlockSpec((d, tm), lambda i: (0, i))]
        + [pl.BlockSpec((nheads, nkeys, tm), tok)] * 4,
        out_shape=[jax.ShapeDtypeStruct((d, nt), BF16), big, big, big, big],
        scratch_shapes=[pltpu.VMEM((wqt.shape[0], tm), F32)],
        compiler_params=_cparams(("arbitrary",)),
        name="peer_route",
    )(xall, g, wqt, keys)


def _expert_kernel(x_ref, xnt_ref, th_ref, c1_ref, s2_ref, e2_ref, u_ref, vt_ref,
                   xo_ref, acc, a_s, w_s, *, jblk):
    c = pl.program_id(1)
    nheads, nkeys, tm = s2_ref.shape
    ec = u_ref.shape[0]
    nrow_i = ec // nkeys
    njb = nkeys // jblk
    slab = nrow_i * jblk
    mm_lanes = 2 * LANE
    i0 = pl.multiple_of(c * nrow_i, SUBLANE)

    @pl.when(c == 0)
    def _():
        acc[...] = jnp.zeros_like(acc)

    def gates(l0, jb):
        lanes = pl.ds(l0, LANE)
        jrows = slice(jb * jblk, (jb + 1) * jblk)
        for first in range(0, nrow_i, GATE_ROWS):
            iis = range(first, first + GATE_ROWS)
            g = {ii: jnp.zeros((jblk, LANE), F32) for ii in iis}
            for h in range(nheads):
                th = th_ref[h, pl.ds(i0, SUBLANE), lanes]
                c1 = c1_ref[h, pl.ds(i0, SUBLANE), lanes]
                s2 = s2_ref[h, jrows, lanes]
                e2 = e2_ref[h, jrows, lanes]
                for ii in iis:
                    picked = jnp.where(s2 >= th[ii:ii + 1, :], e2, 0.0)
                    g[ii] = g[ii] + picked * c1[ii:ii + 1, :]
            for ii in iis:
                rows = slice(jb * slab + ii * jblk, jb * slab + (ii + 1) * jblk)
                w_s[rows, lanes] = (g[ii] * jax.nn.gelu(a_s[rows, lanes])).astype(BF16)

    def piece(l, carry):
        m0 = pl.multiple_of(l * mm_lanes, mm_lanes)
        lanes = pl.ds(m0, mm_lanes)

        for jb in range(njb):
            rows = slice(jb * slab, (jb + 1) * slab)
            a_s[rows, lanes] = jnp.dot(u_ref[rows, :], xnt_ref[:, lanes],
                                       preferred_element_type=F32)
            for sub in range(mm_lanes // LANE):
                gates(pl.multiple_of(m0 + sub * LANE, LANE), jb)
            acc[:, lanes] += jnp.dot(vt_ref[:, rows], w_s[rows, lanes],
                                     preferred_element_type=F32)
        return carry

    lax.fori_loop(0, tm // mm_lanes, piece, 0)

    @pl.when(c == pl.num_programs(1) - 1)
    def _():
        xo_ref[...] = x_ref[...] + acc[...].T


def _expert_call(xall, xnt, th, c1, s2, e2, u, vt):
    nt, d = xall.shape
    nheads, nkeys, _ = th.shape
    nexp = u.shape[0]
    tm, ec = TOK_TILE, EXPERT_CHUNK
    tok = lambda i, c: (0, 0, i)
    return pl.pallas_call(
        functools.partial(_expert_kernel, jblk=GATE_BLOCK),
        grid=(nt // tm, nexp // ec),
        in_specs=[pl.BlockSpec((tm, d), lambda i, c: (i, 0)),
                  pl.BlockSpec((d, tm), lambda i, c: (0, i))]
        + [pl.BlockSpec((nheads, nkeys, tm), tok)] * 4
        + [pl.BlockSpec((ec, d), lambda i, c: (c, 0)),
           pl.BlockSpec((d, ec), lambda i, c: (0, c))],
        out_specs=pl.BlockSpec((tm, d), lambda i, c: (i, 0)),
        out_shape=jax.ShapeDtypeStruct((nt, d), F32),
        scratch_shapes=[pltpu.VMEM((d, tm), F32), pltpu.VMEM((ec, tm), F32),
                        pltpu.VMEM((ec, tm), BF16)],
        input_output_aliases={0: 0},
        compiler_params=_cparams(("arbitrary", "arbitrary")),
        name="peer_experts",
    )(xall, xnt, th, c1, s2, e2, u, vt)


def _slab_order(table, nkeys):
    nexp, d = table.shape
    nrow_i = EXPERT_CHUNK // nkeys
    t = table.reshape(nexp // EXPERT_CHUNK, nrow_i, nkeys // GATE_BLOCK, GATE_BLOCK, d)
    return t.transpose(0, 2, 1, 3, 4).reshape(nexp, d)


def _peer(xall, g, wqt, keys, u, vt):
    xnt, th, c1, s2, e2 = _route_call(xall, g, wqt, keys)
    return _expert_call(xall, xnt, th, c1, s2, e2, u, vt)


def _exact_group_sum(sq, ones_bd):
    hi = sq.astype(BF16)
    r1 = sq - hi.astype(F32)
    mid = r1.astype(BF16)
    lo = (r1 - mid.astype(F32)).astype(BF16)
    dot = lambda t: jnp.dot(t, ones_bd, preferred_element_type=F32)
    return dot(hi) + dot(mid) + dot(lo)


def _head_norm_rope(y, g2, cos, sin_signed, ones_bd, first_half, head_dim):
    out = []
    for n in range(y.shape[1] // LANE):
        blk = y[:, n * LANE:(n + 1) * LANE]
        ms = _exact_group_sum(blk * blk, ones_bd) / head_dim
        z = blk * lax.rsqrt(ms + EPS) * g2
        partner = jnp.where(first_half, pltpu.roll(z, LANE - head_dim // 2, 1),
                            pltpu.roll(z, head_dim // 2, 1))
        out.append(z * cos + partner * sin_signed)
    return jnp.concatenate(out, axis=1)


def _qkv_kernel(x_ref, cos_ref, sin_ref, gkv_ref, gq_ref, wk_ref, wv_ref, wq_ref, gk2_ref,
                gq2_ref, k_ref, v_ref, kb_ref, vt_ref, qt_ref, qb_ref, *, head_dim):
    x = x_ref[...]
    cos = cos_ref[...]
    sin_signed = sin_ref[...]
    lane = lax.broadcasted_iota(jnp.int32, (1, LANE), 1)
    first_half = (lane % head_dim) < head_dim // 2
    r = lax.broadcasted_iota(jnp.int32, (LANE, LANE), 0) // head_dim
    cidx = lax.broadcasted_iota(jnp.int32, (LANE, LANE), 1) // head_dim
    ones_bd = jnp.where(r == cidx, 1.0, 0.0).astype(BF16)

    hk = _rmsnorm(x, gkv_ref[...]).astype(BF16)
    k = _head_norm_rope(jnp.dot(hk, wk_ref[...], preferred_element_type=F32), gk2_ref[...],
                        cos, sin_signed, ones_bd, first_half, head_dim)
    v = jnp.dot(hk, wv_ref[...], preferred_element_type=F32)
    hq = _rmsnorm(x, gq_ref[...]).astype(BF16)
    q = _head_norm_rope(jnp.dot(hq, wq_ref[...], preferred_element_type=F32), gq2_ref[...],
                        cos, sin_signed, ones_bd, first_half, head_dim)
    k_ref[...] = k
    v_ref[...] = v
    kb_ref[...] = k.astype(BF16)
    vt_ref[...] = v.T.astype(BF16)
    qs = q * head_dim ** -0.5
    qt_ref[...] = qs.T.astype(BF16)
    qb_ref[...] = qs.astype(BF16)


def _qkv_call(xall, cos, sin_signed, gkv, gq, wk, wv, wq, gk2, gq2, head_dim):
    nt, d = xall.shape
    tm = TOK_TILE
    row = lambda i: (i, 0)
    ws = [gkv, gq, wk, wv, wq, gk2, gq2]
    return pl.pallas_call(
        functools.partial(_qkv_kernel, head_dim=head_dim),
        grid=(nt // tm,),
        in_specs=[pl.BlockSpec((tm, d), row), pl.BlockSpec((tm, LANE), row),
                  pl.BlockSpec((tm, LANE), row)] + [_full(a.shape) for a in ws],
        out_specs=[pl.BlockSpec((tm, d), row)] * 3 + [pl.BlockSpec((d, tm), lambda i: (0, i))] * 2
        + [pl.BlockSpec((tm, d), row)],
        out_shape=[jax.ShapeDtypeStruct((nt, d), F32)] * 2 + [jax.ShapeDtypeStruct((nt, d), BF16)]
        + [jax.ShapeDtypeStruct((d, nt), BF16)] * 2 + [jax.ShapeDtypeStruct((nt, d), BF16)],
        compiler_params=_cparams(("arbitrary",)),
        name="qkv_proj",
    )(xall, cos, sin_signed, *ws)


def _prompt_attn_kernel(*refs, head_dim, nstream):
    qt_refs = refs[:nstream]
    k_refs = refs[nstream:2 * nstream]
    vt_refs = refs[2 * nstream:3 * nstream]
    a0_ref, a1_ref = refs[3 * nstream:]
    qi = pl.program_id(1)
    tq = qt_refs[0].shape[1]
    tk = tq
    feat = lax.broadcasted_iota(jnp.int32, (LANE, 1), 0)
    qs = []
    for qt_ref in qt_refs:
        qt = qt_ref[...]
        zero = jnp.zeros_like(qt)
        qs.append(jnp.concatenate([jnp.where(feat < head_dim, qt, zero),
                                   jnp.where(feat >= head_dim, qt, zero)], axis=1))

    def update(carry, j, masked):
        k0 = pl.multiple_of(j * tk, tk)
        s = [jnp.dot(k_refs[b][pl.ds(k0, tk), :], qs[b], preferred_element_type=F32)
             for b in range(nstream)]
        out = []
        for b in range(nstream):
            m, l, acc = carry[b]
            sb = s[b]
            if masked:
                kpos = lax.broadcasted_iota(jnp.int32, (tk, 2 * tq), 0)
                qpos = lax.broadcasted_iota(jnp.int32, (tk, 2 * tq), 1) % tq
                sb = jnp.where(kpos <= qpos, sb, NEG)
            m_new = jnp.maximum(m, jnp.max(sb, axis=0, keepdims=True))
            corr = jnp.exp(m - m_new)
            p = jnp.exp(sb - m_new)
            l = l * corr + jnp.sum(p, axis=0, keepdims=True)
            acc = acc * corr + jnp.dot(vt_refs[b][:, pl.ds(k0, tk)], p.astype(BF16),
                                       preferred_element_type=F32)
            out.append((m_new, l, acc))
        return tuple(out)

    init = tuple((jnp.full((1, 2 * tq), NEG, F32), jnp.zeros((1, 2 * tq), F32),
                  jnp.zeros((LANE, 2 * tq), F32)) for _ in range(nstream))
    carry = lax.fori_loop(0, qi, lambda j, cr: update(cr, j, False), init)
    carry = update(carry, qi, True)
    for b, (m, l, acc) in enumerate(carry):
        out = acc / l
        a0_ref[b] = out[:, :tq]
        a1_ref[b] = out[:, tq:]


def _prompt_attn_call(qt, kb, vt, nbatch, tp, head_dim):
    d, nt = qt.shape
    nheads = d // LANE
    tq = ATT_TILE
    nq = tp // tq
    bind = lambda f, b: functools.partial(f, b=b)
    streams = range(nbatch)
    return pl.pallas_call(
        functools.partial(_prompt_attn_kernel, head_dim=head_dim, nstream=nbatch),
        grid=(nheads, nq),
        in_specs=[pl.BlockSpec((LANE, tq), bind(lambda h, i, b: (h, b * nq + i), b))
                  for b in streams]
        + [pl.BlockSpec((tp, LANE), bind(lambda h, i, b: (b, h), b)) for b in streams]
        + [pl.BlockSpec((LANE, tp), bind(lambda h, i, b: (h, b), b)) for b in streams],
        out_specs=[pl.BlockSpec((nbatch, LANE, tq), lambda h, i: (0, h, i))] * 2,
        out_shape=[jax.ShapeDtypeStruct((nbatch, d, tp), F32)] * 2,
        compiler_params=_cparams(("arbitrary", "arbitrary")),
        name="prompt_attention",
    )(*([qt] * nbatch), *([kb] * nbatch), *([vt] * nbatch))


def _decode_attn_kernel(pt_ref, q_ref, kn_ref, vn_ref, *refs, head_dim, npg):
    del pt_ref
    kp = refs[:npg]
    vp = refs[npg:2 * npg]
    o_ref, qblk, expand, hmask, m_s, l_s, acc = refs[2 * npg:]
    p = pl.program_id(1)
    nrow, d = qblk.shape
    nheads = d // LANE
    ntok = nrow // (2 * nheads)
    psz = expand.shape[0]

    @pl.when(p == 0)
    def _():
        q = q_ref[0].astype(F32)
        row = lax.broadcasted_iota(jnp.int32, (nrow, d), 0)
        col = lax.broadcasted_iota(jnp.int32, (nrow, d), 1)
        rep = jnp.concatenate(
            [jnp.broadcast_to(q[t:t + 1, :], (nheads, d)) for t in range(ntok)] * 2, axis=0)
        mine = (col // LANE == row % nheads) & ((col % LANE) // head_dim == row // (ntok * nheads))
        qblk[...] = jnp.where(mine, rep, 0.0).astype(BF16)
        pos = lax.broadcasted_iota(jnp.int32, expand.shape, 0)
        ecol = lax.broadcasted_iota(jnp.int32, expand.shape, 1)
        expand[...] = jnp.where(ecol // nheads == pos, 1.0, 0.0).astype(BF16)
        hrow = lax.broadcasted_iota(jnp.int32, hmask.shape, 0)
        hcol = lax.broadcasted_iota(jnp.int32, hmask.shape, 1)
        hmask[...] = jnp.where(hcol % nheads == hrow % nheads, 1.0, 0.0)
        m_s[...] = jnp.full_like(m_s, NEG)
        l_s[...] = jnp.zeros_like(l_s)
        acc[...] = jnp.zeros_like(acc)

    def update(s, vs):
        m = m_s[...]
        m_new = jnp.maximum(m, jnp.max(s, axis=-1, keepdims=True))
        corr = jnp.exp(m - m_new)
        pr = jnp.exp(s - m_new)
        l_s[...] = l_s[...] * corr + jnp.sum(pr, axis=-1, keepdims=True)
        m_s[...] = m_new
        pv = acc[...] * corr
        for i, v in enumerate(vs):
            spread = jnp.dot(pr[:, i * psz:(i + 1) * psz].astype(BF16), expand[...],
                             preferred_element_type=F32) * hmask[...]
            pv = pv + jnp.dot(spread.astype(BF16), v, preferred_element_type=F32)
        acc[...] = pv

    qb = qblk[...]
    s = jnp.concatenate([jnp.dot(qb, kp[i][0].astype(BF16), preferred_element_type=F32)
                         for i in range(npg)], axis=1)
    update(s, [vp[i][0].astype(BF16) for i in range(npg)])

    @pl.when(p == pl.num_programs(1) - 1)
    def _():
        s = jnp.dot(qblk[...], kn_ref[0].astype(BF16), preferred_element_type=F32)
        trow = (lax.broadcasted_iota(jnp.int32, s.shape, 0) // nheads) % ntok
        tcol = lax.broadcasted_iota(jnp.int32, s.shape, 1)
        update(jnp.where(tcol <= trow, s, NEG), [vn_ref[0].astype(BF16)])
        o_ref[0] = acc[...] / l_s[...]


def _decode_attn_call(page_table, q, knt, vn, cache_kt, cache_v, head_dim):
    nb, ntok, d = q.shape
    npages = page_table.shape[1]
    npg = PAGES_PER_STEP
    psz = cache_kt.shape[2]
    nheads = d // LANE
    nrow = 2 * ntok * nheads
    bmap = lambda b, p, pt: (b, 0, 0)
    pmaps = [functools.partial(lambda b, p, pt, i: (pt[b, p * npg + i], 0, 0), i=i)
             for i in range(npg)]
    grid_spec = pltpu.PrefetchScalarGridSpec(
        num_scalar_prefetch=1,
        grid=(nb, npages // npg),
        in_specs=[pl.BlockSpec((1, ntok, d), bmap), pl.BlockSpec((1, d, psz), bmap),
                  pl.BlockSpec((1, psz * nheads, LANE), bmap)]
        + [pl.BlockSpec((1, d, psz), pm) for pm in pmaps]
        + [pl.BlockSpec((1, psz * nheads, LANE), pm) for pm in pmaps],
        out_specs=pl.BlockSpec((1, nrow, LANE), bmap),
        scratch_shapes=[pltpu.VMEM((nrow, d), BF16),
                        pltpu.VMEM((psz, psz * nheads), BF16),
                        pltpu.VMEM((nrow, psz * nheads), F32),
                        pltpu.VMEM((nrow, 1), F32),
                        pltpu.VMEM((nrow, 1), F32),
                        pltpu.VMEM((nrow, LANE), F32)])
    return pl.pallas_call(
        functools.partial(_decode_attn_kernel, head_dim=head_dim, npg=npg),
        grid_spec=grid_spec,
        out_shape=jax.ShapeDtypeStruct((nb, nrow, LANE), F32),
        compiler_params=_cparams(("arbitrary", "arbitrary")),
        name="decode_attention",
    )(page_table, q, knt, vn, *([cache_kt] * npg), *([cache_v] * npg))


def _combine_kernel(x_ref, a0_ref, a1_ref, lp_ref, gs_ref, wot_ref, xo_ref, *, lambda_init):
    lp = lp_ref[...]
    lam = (jnp.exp(jnp.sum(lp[0:1] * lp[1:2], axis=-1, keepdims=True))
           - jnp.exp(jnp.sum(lp[2:3] * lp[3:4], axis=-1, keepdims=True)) + lambda_init)
    o = a0_ref[...] - lam * a1_ref[...]
    parts = []
    for h in range(o.shape[0] // LANE):
        oh = o[h * LANE:(h + 1) * LANE, :]
        ms = jnp.mean(oh * oh, axis=0, keepdims=True)
        parts.append(oh * lax.rsqrt(ms + EPS) * gs_ref[...] * (1.0 - lambda_init))
    ob = jnp.concatenate(parts, axis=0).astype(BF16)
    yt = jnp.dot(wot_ref[...], ob, preferred_element_type=F32)
    xo_ref[...] = x_ref[...] + yt.T


def _combine_call(xall, a0t, a1t, lp, gs, wot, lambda_init):
    nt, d = xall.shape
    tm = TOK_TILE
    row = lambda i: (i, 0)
    col = lambda i: (0, i)
    return pl.pallas_call(
        functools.partial(_combine_kernel, lambda_init=lambda_init),
        grid=(nt // tm,),
        in_specs=[pl.BlockSpec((tm, d), row), pl.BlockSpec((d, tm), col),
                  pl.BlockSpec((d, tm), col), _full(lp.shape), _full(gs.shape),
                  _full(wot.shape)],
        out_specs=pl.BlockSpec((tm, d), row),
        out_shape=jax.ShapeDtypeStruct((nt, d), F32),
        input_output_aliases={0: 0},
        compiler_params=_cparams(("arbitrary",)),
        name="diff_combine_out_proj",
    )(xall, a0t, a1t, lp, gs, wot)


def _round_up(n, m):
    return -(-n // m) * m


def kernel(x_prompt, x_sample, state_lru_h, state_conv, cache_k, cache_v, page_table, meta_tokens, g_mix, g_ffn, lru_w_gate, lru_w_in, lru_conv_w, lru_conv_b, lru_w_a, lru_b_a, lru_w_x, lru_b_x, lru_lambda, lru_w_out, kv_norm, w_k, w_v, k_norm, w_q, q_norm, lam_params, sub_norm, w_o, peer_w_q, peer_keys, peer_u, peer_v):
    nbatch, seq, d = x_prompt.shape
    nb_dec, dec_seq, _ = x_sample.shape
    depth = g_mix.shape[0]
    assert depth == 2 and lru_w_gate.shape[0] == 1 and w_q.shape[0] == 1
    head_dim = k_norm.shape[0]
    nheads = d // (2 * head_dim)
    assert 2 * head_dim == LANE
    past_len = page_table.shape[1] * cache_k.shape[1]
    t_real = seq + N_META
    tp = _round_up(t_real, math.lcm(LRU_TILE, ATT_TILE))
    ns = nb_dec * dec_seq
    row_s = nbatch * tp
    assert row_s % ns == 0
    nt = _round_up(row_s + ns, math.lcm(TOK_TILE, ROUTE_TILE))
    row2 = lambda a: a.reshape(1, -1)

    xall = jnp.zeros((nt, d), F32)
    for b in range(nbatch):
        xall = lax.dynamic_update_slice(xall, meta_tokens.astype(F32), (b * tp, 0))
        xall = lax.dynamic_update_slice(xall, x_prompt[b], (b * tp + N_META, 0))
    xall = lax.dynamic_update_slice(xall, x_sample.transpose(1, 0, 2).reshape(ns, d), (row_s, 0))

    lw = [row2(g_mix[0]), lru_w_gate[0].astype(BF16), lru_w_in[0].astype(BF16), lru_conv_w[0],
          row2(lru_conv_b[0]), lru_w_a[0].astype(BF16), row2(lru_b_a[0]),
          lru_w_x[0].astype(BF16), row2(lru_b_x[0]), row2(lru_lambda[0]),
          lru_w_out[0].astype(BF16)]
    xall, h_p, buf_p = _lru_call(
        xall, jnp.zeros((nbatch, 1, d), F32), jnp.zeros((nbatch, CONV_W - 1, d), F32), lw,
        nbatch=nbatch, nseq=1, rows=LRU_TILE, ntile=tp // LRU_TILE, row0=0, reset_first=True,
        last_t=t_real - 1)
    buf0_s = state_conv[0].transpose(1, 0, 2).reshape(1, (CONV_W - 1) * nb_dec, d)
    xall, h_s, buf_s = _lru_call(
        xall, state_lru_h[0][None], buf0_s, lw,
        nbatch=1, nseq=nb_dec, rows=ns, ntile=1, row0=row_s, reset_first=False,
        last_t=dec_seq - 1)

    def peer_layer(x, layer):
        keys = peer_keys[layer].reshape(-1, peer_keys.shape[3], peer_keys.shape[4]).astype(BF16)
        nkeys = peer_keys.shape[3]
        return _peer(x, row2(g_ffn[layer]), peer_w_q[layer].T.astype(BF16), keys,
                     _slab_order(peer_u[layer].astype(BF16), nkeys),
                     _slab_order(peer_v[layer].astype(BF16), nkeys).T)

    xall = peer_layer(xall, 0)

    half = head_dim // 2
    inv = ROPE_THETA ** (-jnp.arange(half, dtype=F32) / half)
    rows = jnp.arange(nt)
    pos = jnp.where(rows < row_s, rows % tp,
                    jnp.where(rows < row_s + ns, past_len + (rows - row_s) // nb_dec, 0))
    ang = pos.astype(F32)[:, None] * inv[None, :]
    cos = jnp.tile(jnp.cos(ang), (1, LANE // half))
    sin = jnp.sin(ang)
    sin_signed = jnp.tile(jnp.concatenate([-sin, sin], axis=1), (1, LANE // head_dim))
    tile2 = lambda g: jnp.tile(g, LANE // head_dim).reshape(1, LANE)
    k_all, v_all, kb, vt, qt, qb = _qkv_call(
        xall, cos, sin_signed, row2(kv_norm), row2(g_mix[1]), w_k.astype(BF16),
        w_v.astype(BF16), w_q[0].astype(BF16), tile2(k_norm), tile2(q_norm[0]), head_dim)

    a0p, a1p = _prompt_attn_call(qt, kb, vt, nbatch, tp, head_dim)

    to_seq = lambda a: a[row_s:row_s + ns].reshape(dec_seq, nb_dec, d).transpose(1, 0, 2)
    npool, psz = cache_k.shape[:2]
    pad_tok = ((0, 0), (0, psz - dec_seq), (0, 0))
    knt = jnp.pad(to_seq(k_all), pad_tok).transpose(0, 2, 1)
    vn = jnp.pad(to_seq(v_all), pad_tok).reshape(nb_dec, psz * nheads, LANE)
    o_s = _decode_attn_call(
        page_table, to_seq(qb), knt, vn,
        cache_k.reshape(npool, psz, d).transpose(0, 2, 1),
        cache_v.reshape(npool, psz * nheads, LANE), head_dim)
    o_s = o_s.reshape(nb_dec, 2, dec_seq, d)
    from_seq_t = lambda a: a.transpose(2, 1, 0).reshape(d, ns)
    tail = jnp.zeros((d, nt - row_s - ns), F32)
    columns = lambda ap, a_s: jnp.concatenate(
        [ap[b] for b in range(nbatch)] + [from_seq_t(a_s), tail], axis=1)
    a0t = columns(a0p, o_s[:, 0])
    a1t = columns(a1p, o_s[:, 1])

    lambda_init = 0.8 - 0.6 * math.exp(-0.3 * 1)
    xall = _combine_call(xall, a0t, a1t, lam_params[0], sub_norm[0].reshape(-1, 1),
                         w_o[0].T.astype(BF16), lambda_init)
    xall = peer_layer(xall, 1)

    prompt = lambda a, t0: jnp.stack([a[b * tp + t0:b * tp + t_real] for b in range(nbatch)])
    y_prompt = prompt(xall, N_META)
    y_sample = to_seq(xall)
    k_p = prompt(k_all, 0).reshape(nbatch, t_real, nheads, 2, head_dim)
    v_p = prompt(v_all, 0).reshape(nbatch, t_real, nheads, 2 * head_dim)
    k_s = to_seq(k_all).reshape(nb_dec, dec_seq, nheads, 2, head_dim)
    v_s = to_seq(v_all).reshape(nb_dec, dec_seq, nheads, 2 * head_dim)
    conv_s = buf_s.reshape(CONV_W - 1, nb_dec, d).transpose(1, 0, 2)[None]
    return (y_prompt, y_sample, h_p.reshape(1, nbatch, d), buf_p[None], k_p, v_p,
            h_s, conv_s, k_s, v_s)
```

```python
import functools
import math

import jax
import jax.numpy as jnp
from jax import lax
from jax.experimental import pallas as pl
from jax.experimental.pallas import tpu as pltpu

F32 = jnp.float32
BF16 = jnp.bfloat16

EPS = 1e-6
NEG = -1e30
LRU_C = 8.0
ROPE_THETA = 10000.0
CONV_W = 4
N_META = 16
PEER_TOPK = 16

LANE = 128
SUBLANE = 8
VMEM_LIMIT = 56 * 1024 * 1024

LRU_TILE = 640
ATT_TILE = 640
TOK_TILE = 512
ROUTE_TILE = 256
EXPERT_CHUNK = 1024
GATE_BLOCK = 32
GATE_ROWS = 4
PAGES_PER_STEP = 8


def _cparams(sem):
    return pltpu.CompilerParams(dimension_semantics=sem, vmem_limit_bytes=VMEM_LIMIT)


def _rmsnorm(x, g):
    ms = jnp.mean(x * x, axis=-1, keepdims=True)
    return x * lax.rsqrt(ms + EPS) * g


def _full(shape):
    n = len(shape)
    return pl.BlockSpec(shape, lambda *_: (0,) * n)


def _lru_kernel(x_ref, h0_ref, buf0_ref, g_ref, wg_ref, wi_ref, cw_ref, cb_ref, wa_ref, ba_ref,
                wx_ref, bx_ref, lam_ref, wo_ref,
                xo_ref, ht_ref, buft_ref,
                ubuf, a_s, u_s, hs_s, h_s, *, nseq, rows, reset_first, last_t):
    ti = pl.program_id(1)
    d = x_ref.shape[-1]
    tail = (CONV_W - 1) * nseq
    off = -(-tail // SUBLANE) * SUBLANE

    @pl.when(ti == 0)
    def _():
        h_s[...] = h0_ref[0]
        ubuf[off - tail:off, :] = buf0_ref[0]

    x = x_ref[...]
    xb = _rmsnorm(x, g_ref[...]).astype(BF16)
    gate = jax.nn.gelu(jnp.dot(xb, wg_ref[...], preferred_element_type=F32))
    ubuf[off:off + rows, :] = jnp.dot(xb, wi_ref[...], preferred_element_type=F32)

    y = cb_ref[...]
    for k in range(CONV_W):
        s = off - (CONV_W - 1 - k) * nseq
        y = y + ubuf[s:s + rows, :] * cw_ref[k:k + 1, :]

    steps = rows // nseq
    lt_tile, lt_loc = divmod(last_t, steps)

    @pl.when(ti == lt_tile)
    def _():
        s = off + (lt_loc - (CONV_W - 2)) * nseq
        buft_ref[0] = ubuf[s:s + tail, :]

    ubuf[off - tail:off, :] = ubuf[off + rows - tail:off + rows, :]

    lam = -lam_ref[...]
    softplus = jnp.maximum(lam, 0.0) + jnp.log1p(jnp.exp(-jnp.abs(lam)))
    nblk = wa_ref.shape[0]
    bw = d // nblk
    if reset_first:
        row = lax.broadcasted_iota(jnp.int32, (rows, bw), 0)
        is_t0 = row < jnp.where(ti == 0, nseq, 0)
    for n in range(nblk):
        sl = slice(n * bw, (n + 1) * bw)
        yn = y[:, sl]
        ybn = yn.astype(BF16)
        r = jax.nn.sigmoid(jnp.dot(ybn, wa_ref[n], preferred_element_type=F32) + ba_ref[:, sl])
        i = jax.nn.sigmoid(jnp.dot(ybn, wx_ref[n], preferred_element_type=F32) + bx_ref[:, sl])
        log_a = -LRU_C * r * softplus[:, sl]
        a = jnp.exp(log_a)
        mult = jnp.sqrt(-jnp.tanh(log_a) * (a * a + 1.0))
        if reset_first:
            mult = jnp.where(is_t0, 1.0, mult)
        a_s[:, sl] = a
        u_s[:, sl] = mult * i * yn

    if nseq == 1:
        row8 = lax.broadcasted_iota(jnp.int32, (SUBLANE, d), 0)

        def group(gi, h):
            base = pl.multiple_of(gi * SUBLANE, SUBLANE)
            a8 = a_s[pl.ds(base, SUBLANE), :]
            u8 = u_s[pl.ds(base, SUBLANE), :]
            hs8 = jnp.zeros_like(a8)
            for r_ in range(SUBLANE):
                h = a8[r_:r_ + 1, :] * h + u8[r_:r_ + 1, :]
                hs8 = jnp.where(row8 == r_, h, hs8)
            hs_s[pl.ds(base, SUBLANE), :] = hs8
            return h
        h = lax.fori_loop(0, rows // SUBLANE, group, h_s[...])
    else:
        h = h_s[...]
        for t in range(steps):
            h = a_s[t * nseq:(t + 1) * nseq, :] * h + u_s[t * nseq:(t + 1) * nseq, :]
            hs_s[t * nseq:(t + 1) * nseq, :] = h
    h_s[...] = h

    @pl.when(ti == lt_tile)
    def _():
        ht_ref[0] = hs_s[lt_loc * nseq:(lt_loc + 1) * nseq, :]

    go = (gate * hs_s[...]).astype(BF16)
    xo_ref[...] = x + jnp.dot(go, wo_ref[...], preferred_element_type=F32)


def _lru_call(xall, h0, buf0, w, *, nbatch, nseq, rows, ntile, row0, reset_first, last_t):
    nt, d = xall.shape
    tail = (CONV_W - 1) * nseq
    off = -(-tail // SUBLANE) * SUBLANE
    blk0 = row0 // rows
    xmap = lambda b, t: (blk0 + b * ntile + t, 0)
    smap = lambda b, t: (b, 0, 0)
    kern = functools.partial(_lru_kernel, nseq=nseq, rows=rows, reset_first=reset_first,
                             last_t=last_t)
    return pl.pallas_call(
        kern,
        grid=(nbatch, ntile),
        in_specs=[pl.BlockSpec((rows, d), xmap),
                  pl.BlockSpec((1, nseq, d), smap),
                  pl.BlockSpec((1, tail, d), smap)] + [_full(a.shape) for a in w],
        out_specs=[pl.BlockSpec((rows, d), xmap),
                   pl.BlockSpec((1, nseq, d), smap),
                   pl.BlockSpec((1, tail, d), smap)],
        out_shape=[jax.ShapeDtypeStruct((nt, d), F32),
                   jax.ShapeDtypeStruct((nbatch, nseq, d), F32),
                   jax.ShapeDtypeStruct((nbatch, tail, d), F32)],
        scratch_shapes=[pltpu.VMEM((off + rows, d), F32),
                        pltpu.VMEM((rows, d), F32),
                        pltpu.VMEM((rows, d), F32),
                        pltpu.VMEM((rows, d), F32),
                        pltpu.VMEM((nseq, d), F32)],
        input_output_aliases={0: 0},
        compiler_params=_cparams(("arbitrary", "arbitrary")),
        name="rg_lru_block",
    )(xall, h0, buf0, *w)


def _top_values(v, n, masked=0):
    tops = []
    for _ in range(n):
        m = jnp.max(v, axis=0, keepdims=True)
        tops.append(m)
        v = jnp.where(v == m, -jnp.inf, v)
    removed = jnp.sum(jnp.where(v == -jnp.inf, 1.0, 0.0), axis=0, keepdims=True) - float(masked)
    return tops, removed


def _candidate_sums(a, b):
    k = len(a)
    assert k == 2 * SUBLANE
    t = a[0].shape[1]
    row16 = lax.broadcasted_iota(jnp.int32, (k, t), 0)
    row8 = lax.broadcasted_iota(jnp.int32, (SUBLANE, t), 0)
    b16 = jnp.zeros((k, t), F32)
    a_hi = jnp.zeros((SUBLANE, t), F32)
    for i in range(k):
        b16 = jnp.where(row16 == i, b[i], b16)
    for i in range(SUBLANE):
        a_hi = jnp.where(row8 == i, a[SUBLANE + i], a_hi)
    b8 = b16[:SUBLANE]
    parts = [a[0] + b16]
    masked = 0
    for p in range(1, SUBLANE):
        limit = k // (p + 1)
        parts.append(jnp.where(row8 < limit, a[p] + b8, -jnp.inf))
        masked += SUBLANE - limit
    parts.append(a_hi + b[0])
    return jnp.concatenate(parts, axis=0), masked


_NO_RANK = 1024.0


def _ranked_top(v, k):
    index = lax.broadcasted_iota(jnp.int32, v.shape, 0).astype(F32)
    rank = jnp.full(v.shape, _NO_RANK, F32)
    vals = []
    for t in range(k):
        m = jnp.max(v, axis=0, keepdims=True)
        first = jnp.min(jnp.where(v == m, index, float(v.shape[0])), axis=0, keepdims=True)
        hit = index == first
        rank = jnp.where(hit, float(t + 1), rank)
        vals.append(m)
        v = jnp.where(hit, -jnp.inf, v)
    return rank, vals


def _ranked_row_counts(a, b):
    k = len(a)
    t = a[0].shape[1]
    row = lax.broadcasted_iota(jnp.int32, (k, t), 0).astype(F32)
    b_col = jnp.zeros((k, t), F32)
    for q in range(k):
        b_col = jnp.where(row == float(q), b[q], b_col)
    cand = jnp.concatenate([a[p] + b_col for p in range(k)], axis=0)
    flat = lax.broadcasted_iota(jnp.int32, cand.shape, 0).astype(F32)
    taken = jnp.zeros((k, t), F32)
    z = jnp.zeros((1, t), F32)
    best = None
    for _ in range(k):
        m = jnp.max(cand, axis=0, keepdims=True)
        first = jnp.min(jnp.where(cand == m, flat, float(k * k)), axis=0, keepdims=True)
        taken = taken + jnp.where(row == jnp.floor(first / float(k)), 1.0, 0.0)
        best = m if best is None else best
        z = z + jnp.exp(m - best)
        cand = jnp.where(flat == first, -jnp.inf, cand)
    return taken, z


def _route_kernel(x_ref, g_ref, wqt_ref, keys_ref,
                  xnt_ref, th_ref, c1_ref, s2_ref, e2_ref, q_s):
    nheads = th_ref.shape[0]
    nkeys = th_ref.shape[1]
    xn = _rmsnorm(x_ref[...], g_ref[...])
    xnt = xn.T.astype(BF16)
    xnt_ref[...] = xnt
    q_s[...] = jnp.dot(wqt_ref[...], xnt, preferred_element_type=F32)

    def head(h, carry):
        tops, dup, expd, sc = [], [], [], []
        for c in range(2):
            r0 = pl.multiple_of((h * 2 + c) * nkeys, nkeys)
            qhc = q_s[pl.ds(r0, nkeys), :].astype(BF16)
            s = jnp.dot(keys_ref[h * 2 + c], qhc, preferred_element_type=F32)
            t, removed = _top_values(s, PEER_TOPK + 1)
            tops.append(t)
            dup.append(removed != float(PEER_TOPK + 1))
            expd.append(jnp.exp(s - t[0]))
            sc.append(s)
        cand, masked = _candidate_sums(tops[0][:PEER_TOPK], tops[1][:PEER_TOPK])
        best, removed = _top_values(cand, PEER_TOPK, masked)
        tau = best[PEER_TOPK - 1]
        z = jnp.ones_like(tau)
        for t in range(1, PEER_TOPK):
            z = z + jnp.exp(best[t] - best[0])
        tie = dup[0] | dup[1] | (removed != float(PEER_TOPK))
        row16 = lax.broadcasted_iota(jnp.int32, (PEER_TOPK, tau.shape[1]), 0)
        b16 = jnp.zeros((PEER_TOPK, tau.shape[1]), F32)
        for q in range(PEER_TOPK):
            b16 = jnp.where(row16 == q, tops[1][q], b16)
        th = jnp.full(sc[0].shape, jnp.inf, F32)
        for p in range(PEER_TOPK):
            a_p = tops[0][p]
            theta = jnp.min(jnp.where(a_p + b16 >= tau, b16, jnp.inf), axis=0, keepdims=True)
            th = jnp.where(sc[0] == a_p, theta, th)
        th_ref[h] = th
        c1_ref[h] = expd[0] / z
        s2_ref[h] = sc[1]
        e2_ref[h] = expd[1]

        @pl.when(jnp.max(jnp.where(tie, 1.0, 0.0)) > 0.0)
        def _():
            rank1, a = _ranked_top(sc[0], PEER_TOPK)
            rank2, b = _ranked_top(sc[1], PEER_TOPK)
            taken, z_ranked = _ranked_row_counts(a, b)
            width = jnp.zeros_like(sc[0])
            for p in range(PEER_TOPK):
                width = jnp.where(rank1 == float(p + 1), taken[p:p + 1, :], width)
            th_ref[h] = jnp.where(tie, -width, th)
            s2_ref[h] = jnp.where(tie, -rank2, sc[1])
            c1_ref[h] = expd[0] / jnp.where(tie, z_ranked, z)

        return carry

    lax.fori_loop(0, nheads, head, 0)


def _route_call(xall, g, wqt, keys):
    nt, d = xall.shape
    nhc, nkeys, _ = keys.shape
    nheads = nhc // 2
    tm = ROUTE_TILE
    tok = lambda i: (0, 0, i)
    big = jax.ShapeDtypeStruct((nheads, nkeys, nt), F32)
    return pl.pallas_call(
        _route_kernel,
        grid=(nt // tm,),
        in_specs=[pl.BlockSpec((tm, d), lambda i: (i, 0)), _full(g.shape), _full(wqt.shape),
                  _full(keys.shape)],
        out_specs=[pl.BlockSpec((d, tm), lambda i: (0, i))]
        + [pl.BlockSpec((nheads, nkeys, tm), tok)] * 4,
        out_shape=[jax.ShapeDtypeStruct((d, nt), BF16), big, big, big, big],
        scratch_shapes=[pltpu.VMEM((wqt.shape[0], tm), F32)],
        compiler_params=_cparams(("arbitrary",)),
        name="peer_route",
    )(xall, g, wqt, keys)


def _expert_kernel(x_ref, xnt_ref, th_ref, c1_ref, s2_ref, e2_ref, u_ref, vt_ref,
                   xo_ref, acc, a_s, w_s, *, jblk):
    c = pl.program_id(1)
    nheads, nkeys, tm = s2_ref.shape
    ec = u_ref.shape[0]
    nrow_i = ec // nkeys
    njb = nkeys // jblk
    slab = nrow_i * jblk
    mm_lanes = 2 * LANE
    i0 = pl.multiple_of(c * nrow_i, SUBLANE)

    @pl.when(c == 0)
    def _():
        acc[...] = jnp.zeros_like(acc)

    def gates(l0, jb):
        lanes = pl.ds(l0, LANE)
        jrows = slice(jb * jblk, (jb + 1) * jblk)
        for first in range(0, nrow_i, GATE_ROWS):
            iis = range(first, first + GATE_ROWS)
            g = {ii: jnp.zeros((jblk, LANE), F32) for ii in iis}
            for h in range(nheads):
                th = th_ref[h, pl.ds(i0, SUBLANE), lanes]
                c1 = c1_ref[h, pl.ds(i0, SUBLANE), lanes]
                s2 = s2_ref[h, jrows, lanes]
                e2 = e2_ref[h, jrows, lanes]
                for ii in iis:
                    picked = jnp.where(s2 >= th[ii:ii + 1, :], e2, 0.0)
                    g[ii] = g[ii] + picked * c1[ii:ii + 1, :]
            for ii in iis:
                rows = slice(jb * slab + ii * jblk, jb * slab + (ii + 1) * jblk)
                w_s[rows, lanes] = (g[ii] * jax.nn.gelu(a_s[rows, lanes])).astype(BF16)

    def piece(l, carry):
        m0 = pl.multiple_of(l * mm_lanes, mm_lanes)
        lanes = pl.ds(m0, mm_lanes)

        for jb in range(njb):
            rows = slice(jb * slab, (jb + 1) * slab)
            a_s[rows, lanes] = jnp.dot(u_ref[rows, :], xnt_ref[:, lanes],
                                       preferred_element_type=F32)
            for sub in range(mm_lanes // LANE):
                gates(pl.multiple_of(m0 + sub * LANE, LANE), jb)
            acc[:, lanes] += jnp.dot(vt_ref[:, rows], w_s[rows, lanes],
                                     preferred_element_type=F32)
        return carry

    lax.fori_loop(0, tm // mm_lanes, piece, 0)

    @pl.when(c == pl.num_programs(1) - 1)
    def _():
        xo_ref[...] = x_ref[...] + acc[...].T


def _expert_call(xall, xnt, th, c1, s2, e2, u, vt):
    nt, d = xall.shape
    nheads, nkeys, _ = th.shape
    nexp = u.shape[0]
    tm, ec = TOK_TILE, EXPERT_CHUNK
    tok = lambda i, c: (0, 0, i)
    return pl.pallas_call(
        functools.partial(_expert_kernel, jblk=GATE_BLOCK),
        grid=(nt // tm, nexp // ec),
        in_specs=[pl.BlockSpec((tm, d), lambda i, c: (i, 0)),
                  pl.BlockSpec((d, tm), lambda i, c: (0, i))]
        + [pl.BlockSpec((nheads, nkeys, tm), tok)] * 4
        + [pl.BlockSpec((ec, d), lambda i, c: (c, 0)),
           pl.BlockSpec((d, ec), lambda i, c: (0, c))],
        out_specs=pl.BlockSpec((tm, d), lambda i, c: (i, 0)),
        out_shape=jax.ShapeDtypeStruct((nt, d), F32),
        scratch_shapes=[pltpu.VMEM((d, tm), F32), pltpu.VMEM((ec, tm), F32),
                        pltpu.VMEM((ec, tm), BF16)],
        input_output_aliases={0: 0},
        compiler_params=_cparams(("arbitrary", "arbitrary")),
        name="peer_experts",
    )(xall, xnt, th, c1, s2, e2, u, vt)


def _slab_order(table, nkeys):
    nexp, d = table.shape
    nrow_i = EXPERT_CHUNK // nkeys
    t = table.reshape(nexp // EXPERT_CHUNK, nrow_i, nkeys // GATE_BLOCK, GATE_BLOCK, d)
    return t.transpose(0, 2, 1, 3, 4).reshape(nexp, d)


def _peer(xall, g, wqt, keys, u, vt):
    xnt, th, c1, s2, e2 = _route_call(xall, g, wqt, keys)
    return _expert_call(xall, xnt, th, c1, s2, e2, u, vt)


def _exact_group_sum(sq, ones_bd):
    hi = sq.astype(BF16)
    r1 = sq - hi.astype(F32)
    mid = r1.astype(BF16)
    lo = (r1 - mid.astype(F32)).astype(BF16)
    dot = lambda t: jnp.dot(t, ones_bd, preferred_element_type=F32)
    return dot(hi) + dot(mid) + dot(lo)


def _head_norm_rope(y, g2, cos, sin_signed, ones_bd, first_half, head_dim):
    out = []
    for n in range(y.shape[1] // LANE):
        blk = y[:, n * LANE:(n + 1) * LANE]
        ms = _exact_group_sum(blk * blk, ones_bd) / head_dim
        z = blk * lax.rsqrt(ms + EPS) * g2
        partner = jnp.where(first_half, pltpu.roll(z, LANE - head_dim // 2, 1),
                            pltpu.roll(z, head_dim // 2, 1))
        out.append(z * cos + partner * sin_signed)
    return jnp.concatenate(out, axis=1)


def _qkv_kernel(x_ref, cos_ref, sin_ref, gkv_ref, gq_ref, wk_ref, wv_ref, wq_ref, gk2_ref,
                gq2_ref, k_ref, v_ref, kb_ref, vt_ref, qt_ref, qb_ref, *, head_dim):
    x = x_ref[...]
    cos = cos_ref[...]
    sin_signed = sin_ref[...]
    lane = lax.broadcasted_iota(jnp.int32, (1, LANE), 1)
    first_half = (lane % head_dim) < head_dim // 2
    r = lax.broadcasted_iota(jnp.int32, (LANE, LANE), 0) // head_dim
    cidx = lax.broadcasted_iota(jnp.int32, (LANE, LANE), 1) // head_dim
    ones_bd = jnp.where(r == cidx, 1.0, 0.0).astype(BF16)

    hk = _rmsnorm(x, gkv_ref[...]).astype(BF16)
    k = _head_norm_rope(jnp.dot(hk, wk_ref[...], preferred_element_type=F32), gk2_ref[...],
                        cos, sin_signed, ones_bd, first_half, head_dim)
    v = jnp.dot(hk, wv_ref[...], preferred_element_type=F32)
    hq = _rmsnorm(x, gq_ref[...]).astype(BF16)
    q = _head_norm_rope(jnp.dot(hq, wq_ref[...], preferred_element_type=F32), gq2_ref[...],
                        cos, sin_signed, ones_bd, first_half, head_dim)
    k_ref[...] = k
    v_ref[...] = v
    kb_ref[...] = k.astype(BF16)
    vt_ref[...] = v.T.astype(BF16)
    qs = q * head_dim ** -0.5
    qt_ref[...] = qs.T.astype(BF16)
    qb_ref[...] = qs.astype(BF16)


def _qkv_call(xall, cos, sin_signed, gkv, gq, wk, wv, wq, gk2, gq2, head_dim):
    nt, d = xall.shape
    tm = TOK_TILE
    row = lambda i: (i, 0)
    ws = [gkv, gq, wk, wv, wq, gk2, gq2]
    return pl.pallas_call(
        functools.partial(_qkv_kernel, head_dim=head_dim),
        grid=(nt // tm,),
        in_specs=[pl.BlockSpec((tm, d), row), pl.BlockSpec((tm, LANE), row),
                  pl.BlockSpec((tm, LANE), row)] + [_full(a.shape) for a in ws],
        out_specs=[pl.BlockSpec((tm, d), row)] * 3 + [pl.BlockSpec((d, tm), lambda i: (0, i))] * 2
        + [pl.BlockSpec((tm, d), row)],
        out_shape=[jax.ShapeDtypeStruct((nt, d), F32)] * 2 + [jax.ShapeDtypeStruct((nt, d), BF16)]
        + [jax.ShapeDtypeStruct((d, nt), BF16)] * 2 + [jax.ShapeDtypeStruct((nt, d), BF16)],
        compiler_params=_cparams(("arbitrary",)),
        name="qkv_proj",
    )(xall, cos, sin_signed, *ws)


def _prompt_attn_kernel(*refs, head_dim, nstream):
    qt_refs = refs[:nstream]
    k_refs = refs[nstream:2 * nstream]
    vt_refs = refs[2 * nstream:3 * nstream]
    a0_ref, a1_ref = refs[3 * nstream:]
    qi = pl.program_id(1)
    tq = qt_refs[0].shape[1]
    tk = tq
    feat = lax.broadcasted_iota(jnp.int32, (LANE, 1), 0)
    qs = []
    for qt_ref in qt_refs:
        qt = qt_ref[...]
        zero = jnp.zeros_like(qt)
        qs.append(jnp.concatenate([jnp.where(feat < head_dim, qt, zero),
                                   jnp.where(feat >= head_dim, qt, zero)], axis=1))

    def update(carry, j, masked):
        k0 = pl.multiple_of(j * tk, tk)
        s = [jnp.dot(k_refs[b][pl.ds(k0, tk), :], qs[b], preferred_element_type=F32)
             for b in range(nstream)]
        out = []
        for b in range(nstream):
            m, l, acc = carry[b]
            sb = s[b]
            if masked:
                kpos = lax.broadcasted_iota(jnp.int32, (tk, 2 * tq), 0)
                qpos = lax.broadcasted_iota(jnp.int32, (tk, 2 * tq), 1) % tq
                sb = jnp.where(kpos <= qpos, sb, NEG)
            m_new = jnp.maximum(m, jnp.max(sb, axis=0, keepdims=True))
            corr = jnp.exp(m - m_new)
            p = jnp.exp(sb - m_new)
            l = l * corr + jnp.sum(p, axis=0, keepdims=True)
            acc = acc * corr + jnp.dot(vt_refs[b][:, pl.ds(k0, tk)], p.astype(BF16),
                                       preferred_element_type=F32)
            out.append((m_new, l, acc))
        return tuple(out)

    init = tuple((jnp.full((1, 2 * tq), NEG, F32), jnp.zeros((1, 2 * tq), F32),
                  jnp.zeros((LANE, 2 * tq), F32)) for _ in range(nstream))
    carry = lax.fori_loop(0, qi, lambda j, cr: update(cr, j, False), init)
    carry = update(carry, qi, True)
    for b, (m, l, acc) in enumerate(carry):
        out = acc / l
        a0_ref[b] = out[:, :tq]
        a1_ref[b] = out[:, tq:]


def _prompt_attn_call(qt, kb, vt, nbatch, tp, head_dim):
    d, nt = qt.shape
    nheads = d // LANE
    tq = ATT_TILE
    nq = tp // tq
    bind = lambda f, b: functools.partial(f, b=b)
    streams = range(nbatch)
    return pl.pallas_call(
        functools.partial(_prompt_attn_kernel, head_dim=head_dim, nstream=nbatch),
        grid=(nheads, nq),
        in_specs=[pl.BlockSpec((LANE, tq), bind(lambda h, i, b: (h, b * nq + i), b))
                  for b in streams]
        + [pl.BlockSpec((tp, LANE), bind(lambda h, i, b: (b, h), b)) for b in streams]
        + [pl.BlockSpec((LANE, tp), bind(lambda h, i, b: (h, b), b)) for b in streams],
        out_specs=[pl.BlockSpec((nbatch, LANE, tq), lambda h, i: (0, h, i))] * 2,
        out_shape=[jax.ShapeDtypeStruct((nbatch, d, tp), F32)] * 2,
        compiler_params=_cparams(("arbitrary", "arbitrary")),
        name="prompt_attention",
    )(*([qt] * nbatch), *([kb] * nbatch), *([vt] * nbatch))


def _decode_attn_kernel(pt_ref, q_ref, kn_ref, vn_ref, *refs, head_dim, npg):
    del pt_ref
    kp = refs[:npg]
    vp = refs[npg:2 * npg]
    o_ref, qblk, m_s, l_s, acc = refs[2 * npg:]
    p = pl.program_id(1)
    nrow, d = qblk.shape
    nheads = d // LANE
    per_head = nrow // nheads
    ntok = per_head // 2
    psz = kn_ref.shape[2]

    @pl.when(p == 0)
    def _():
        q = q_ref[0].astype(F32)
        row = lax.broadcasted_iota(jnp.int32, (nrow, d), 0)
        col = lax.broadcasted_iota(jnp.int32, (nrow, d), 1)
        rep = jnp.zeros((nrow, d), F32)
        for t in range(ntok):
            rep = jnp.where(row % ntok == t, q[t:t + 1, :], rep)
        mine = (col // LANE == row // per_head) & ((col % LANE) // head_dim == (row // ntok) % 2)
        qblk[...] = jnp.where(mine, rep, 0.0).astype(BF16)
        m_s[...] = jnp.full_like(m_s, NEG)
        l_s[...] = jnp.zeros_like(l_s)
        acc[...] = jnp.zeros_like(acc)

    def update(s, v_refs):
        m = m_s[...]
        m_new = jnp.maximum(m, jnp.max(s, axis=-1, keepdims=True))
        corr = jnp.exp(m - m_new)
        pr = jnp.exp(s - m_new)
        l_s[...] = l_s[...] * corr + jnp.sum(pr, axis=-1, keepdims=True)
        m_s[...] = m_new
        pr = pr.astype(BF16)
        for h in range(nheads):
            rows = slice(h * per_head, (h + 1) * per_head)
            pv = acc[rows, :] * corr[rows, :]
            for i, v_ref in enumerate(v_refs):
                vh = v_ref[0, pl.ds(h, psz, stride=nheads), :].astype(BF16)
                pv = pv + jnp.dot(pr[rows, i * psz:(i + 1) * psz], vh,
                                  preferred_element_type=F32)
            acc[rows, :] = pv

    qb = qblk[...]
    s = jnp.concatenate([jnp.dot(qb, kp[i][0].astype(BF16), preferred_element_type=F32)
                         for i in range(npg)], axis=1)
    update(s, vp)

    @pl.when(p == pl.num_programs(1) - 1)
    def _():
        s = jnp.dot(qblk[...], kn_ref[0].astype(BF16), preferred_element_type=F32)
        trow = lax.broadcasted_iota(jnp.int32, s.shape, 0) % ntok
        tcol = lax.broadcasted_iota(jnp.int32, s.shape, 1)
        update(jnp.where(tcol <= trow, s, NEG), [vn_ref])
        o_ref[0] = acc[...] / l_s[...]


def _decode_attn_call(page_table, q, knt, vn, cache_kt, cache_v, head_dim):
    nb, ntok, d = q.shape
    npages = page_table.shape[1]
    npg = PAGES_PER_STEP
    psz = cache_kt.shape[2]
    nheads = d // LANE
    nrow = 2 * ntok * nheads
    bmap = lambda b, p, pt: (b, 0, 0)
    pmaps = [functools.partial(lambda b, p, pt, i: (pt[b, p * npg + i], 0, 0), i=i)
             for i in range(npg)]
    grid_spec = pltpu.PrefetchScalarGridSpec(
        num_scalar_prefetch=1,
        grid=(nb, npages // npg),
        in_specs=[pl.BlockSpec((1, ntok, d), bmap), pl.BlockSpec((1, d, psz), bmap),
                  pl.BlockSpec((1, psz * nheads, LANE), bmap)]
        + [pl.BlockSpec((1, d, psz), pm) for pm in pmaps]
        + [pl.BlockSpec((1, psz * nheads, LANE), pm) for pm in pmaps],
        out_specs=pl.BlockSpec((1, nrow, LANE), bmap),
        scratch_shapes=[pltpu.VMEM((nrow, d), BF16),
                        pltpu.VMEM((nrow, 1), F32),
                        pltpu.VMEM((nrow, 1), F32),
                        pltpu.VMEM((nrow, LANE), F32)])
    return pl.pallas_call(
        functools.partial(_decode_attn_kernel, head_dim=head_dim, npg=npg),
        grid_spec=grid_spec,
        out_shape=jax.ShapeDtypeStruct((nb, nrow, LANE), F32),
        compiler_params=_cparams(("arbitrary", "arbitrary")),
        name="decode_attention",
    )(page_table, q, knt, vn, *([cache_kt] * npg), *([cache_v] * npg))


def _combine_kernel(x_ref, a0_ref, a1_ref, lp_ref, gs_ref, wot_ref, xo_ref, *, lambda_init):
    lp = lp_ref[...]
    lam = (jnp.exp(jnp.sum(lp[0:1] * lp[1:2], axis=-1, keepdims=True))
           - jnp.exp(jnp.sum(lp[2:3] * lp[3:4], axis=-1, keepdims=True)) + lambda_init)
    o = a0_ref[...] - lam * a1_ref[...]
    parts = []
    for h in range(o.shape[0] // LANE):
        oh = o[h * LANE:(h + 1) * LANE, :]
        ms = jnp.mean(oh * oh, axis=0, keepdims=True)
        parts.append(oh * lax.rsqrt(ms + EPS) * gs_ref[...] * (1.0 - lambda_init))
    ob = jnp.concatenate(parts, axis=0).astype(BF16)
    yt = jnp.dot(wot_ref[...], ob, preferred_element_type=F32)
    xo_ref[...] = x_ref[...] + yt.T


def _combine_call(xall, a0t, a1t, lp, gs, wot, lambda_init):
    nt, d = xall.shape
    tm = TOK_TILE
    row = lambda i: (i, 0)
    col = lambda i: (0, i)
    return pl.pallas_call(
        functools.partial(_combine_kernel, lambda_init=lambda_init),
        grid=(nt // tm,),
        in_specs=[pl.BlockSpec((tm, d), row), pl.BlockSpec((d, tm), col),
                  pl.BlockSpec((d, tm), col), _full(lp.shape), _full(gs.shape),
                  _full(wot.shape)],
        out_specs=pl.BlockSpec((tm, d), row),
        out_shape=jax.ShapeDtypeStruct((nt, d), F32),
        input_output_aliases={0: 0},
        compiler_params=_cparams(("arbitrary",)),
        name="diff_combine_out_proj",
    )(xall, a0t, a1t, lp, gs, wot)


def _round_up(n, m):
    return -(-n // m) * m


def kernel(x_prompt, x_sample, state_lru_h, state_conv, cache_k, cache_v, page_table, meta_tokens, g_mix, g_ffn, lru_w_gate, lru_w_in, lru_conv_w, lru_conv_b, lru_w_a, lru_b_a, lru_w_x, lru_b_x, lru_lambda, lru_w_out, kv_norm, w_k, w_v, k_norm, w_q, q_norm, lam_params, sub_norm, w_o, peer_w_q, peer_keys, peer_u, peer_v):
    nbatch, seq, d = x_prompt.shape
    nb_dec, dec_seq, _ = x_sample.shape
    depth = g_mix.shape[0]
    assert depth == 2 and lru_w_gate.shape[0] == 1 and w_q.shape[0] == 1
    head_dim = k_norm.shape[0]
    nheads = d // (2 * head_dim)
    assert 2 * head_dim == LANE
    past_len = page_table.shape[1] * cache_k.shape[1]
    t_real = seq + N_META
    tp = _round_up(t_real, math.lcm(LRU_TILE, ATT_TILE))
    ns = nb_dec * dec_seq
    row_s = nbatch * tp
    assert row_s % ns == 0
    nt = _round_up(row_s + ns, math.lcm(TOK_TILE, ROUTE_TILE))
    row2 = lambda a: a.reshape(1, -1)

    xall = jnp.zeros((nt, d), F32)
    for b in range(nbatch):
        xall = lax.dynamic_update_slice(xall, meta_tokens.astype(F32), (b * tp, 0))
        xall = lax.dynamic_update_slice(xall, x_prompt[b], (b * tp + N_META, 0))
    xall = lax.dynamic_update_slice(xall, x_sample.transpose(1, 0, 2).reshape(ns, d), (row_s, 0))

    lw = [row2(g_mix[0]), lru_w_gate[0].astype(BF16), lru_w_in[0].astype(BF16), lru_conv_w[0],
          row2(lru_conv_b[0]), lru_w_a[0].astype(BF16), row2(lru_b_a[0]),
          lru_w_x[0].astype(BF16), row2(lru_b_x[0]), row2(lru_lambda[0]),
          lru_w_out[0].astype(BF16)]
    xall, h_p, buf_p = _lru_call(
        xall, jnp.zeros((nbatch, 1, d), F32), jnp.zeros((nbatch, CONV_W - 1, d), F32), lw,
        nbatch=nbatch, nseq=1, rows=LRU_TILE, ntile=tp // LRU_TILE, row0=0, reset_first=True,
        last_t=t_real - 1)
    buf0_s = state_conv[0].transpose(1, 0, 2).reshape(1, (CONV_W - 1) * nb_dec, d)
    xall, h_s, buf_s = _lru_call(
        xall, state_lru_h[0][None], buf0_s, lw,
        nbatch=1, nseq=nb_dec, rows=ns, ntile=1, row0=row_s, reset_first=False,
        last_t=dec_seq - 1)

    def peer_layer(x, layer):
        keys = peer_keys[layer].reshape(-1, peer_keys.shape[3], peer_keys.shape[4]).astype(BF16)
        nkeys = peer_keys.shape[3]
        return _peer(x, row2(g_ffn[layer]), peer_w_q[layer].T.astype(BF16), keys,
                     _slab_order(peer_u[layer].astype(BF16), nkeys),
                     _slab_order(peer_v[layer].astype(BF16), nkeys).T)

    xall = peer_layer(xall, 0)

    half = head_dim // 2
    inv = ROPE_THETA ** (-jnp.arange(half, dtype=F32) / half)
    rows = jnp.arange(nt)
    pos = jnp.where(rows < row_s, rows % tp,
                    jnp.where(rows < row_s + ns, past_len + (rows - row_s) // nb_dec, 0))
    ang = pos.astype(F32)[:, None] * inv[None, :]
    cos = jnp.tile(jnp.cos(ang), (1, LANE // half))
    sin = jnp.sin(ang)
    sin_signed = jnp.tile(jnp.concatenate([-sin, sin], axis=1), (1, LANE // head_dim))
    tile2 = lambda g: jnp.tile(g, LANE // head_dim).reshape(1, LANE)
    k_all, v_all, kb, vt, qt, qb = _qkv_call(
        xall, cos, sin_signed, row2(kv_norm), row2(g_mix[1]), w_k.astype(BF16),
        w_v.astype(BF16), w_q[0].astype(BF16), tile2(k_norm), tile2(q_norm[0]), head_dim)

    a0p, a1p = _prompt_attn_call(qt, kb, vt, nbatch, tp, head_dim)

    to_seq = lambda a: a[row_s:row_s + ns].reshape(dec_seq, nb_dec, d).transpose(1, 0, 2)
    npool, psz = cache_k.shape[:2]
    pad_tok = ((0, 0), (0, psz - dec_seq), (0, 0))
    knt = jnp.pad(to_seq(k_all), pad_tok).transpose(0, 2, 1)
    vn = jnp.pad(to_seq(v_all), pad_tok).reshape(nb_dec, psz * nheads, LANE)
    o_s = _decode_attn_call(
        page_table, to_seq(qb), knt, vn,
        cache_k.reshape(npool, psz, d).transpose(0, 2, 1),
        cache_v.reshape(npool, psz * nheads, LANE), head_dim)
    o_s = o_s.reshape(nb_dec, nheads, 2, dec_seq, LANE).transpose(0, 2, 3, 1, 4)
    o_s = o_s.reshape(nb_dec, 2, dec_seq, d)
    from_seq_t = lambda a: a.transpose(2, 1, 0).reshape(d, ns)
    tail = jnp.zeros((d, nt - row_s - ns), F32)
    columns = lambda ap, a_s: jnp.concatenate(
        [ap[b] for b in range(nbatch)] + [from_seq_t(a_s), tail], axis=1)
    a0t = columns(a0p, o_s[:, 0])
    a1t = columns(a1p, o_s[:, 1])

    lambda_init = 0.8 - 0.6 * math.exp(-0.3 * 1)
    xall = _combine_call(xall, a0t, a1t, lam_params[0], sub_norm[0].reshape(-1, 1),
                         w_o[0].T.astype(BF16), lambda_init)
    xall = peer_layer(xall, 1)

    prompt = lambda a, t0: jnp.stack([a[b * tp + t0:b * tp + t_real] for b in range(nbatch)])
    y_prompt = prompt(xall, N_META)
    y_sample = to_seq(xall)
    k_p = prompt(k_all, 0).reshape(nbatch, t_real, nheads, 2, head_dim)
    v_p = prompt(v_all, 0).reshape(nbatch, t_real, nheads, 2 * head_dim)
    k_s = to_seq(k_all).reshape(nb_dec, dec_seq, nheads, 2, head_dim)
    v_s = to_seq(v_all).reshape(nb_dec, dec_seq, nheads, 2 * head_dim)
    conv_s = buf_s.reshape(CONV_W - 1, nb_dec, d).transpose(1, 0, 2)[None]
    return (y_prompt, y_sample, h_p.reshape(1, nbatch, d), buf_p[None], k_p, v_p,
            h_s, conv_s, k_s, v_s)
```

```python
import functools
import math

import jax
import jax.numpy as jnp
from jax import lax
from jax.experimental import pallas as pl
from jax.experimental.pallas import tpu as pltpu

F32 = jnp.float32
BF16 = jnp.bfloat16

EPS = 1e-6
NEG = -1e30
LRU_C = 8.0
ROPE_THETA = 10000.0
CONV_W = 4
N_META = 16
PEER_TOPK = 16

LANE = 128
SUBLANE = 8
VMEM_LIMIT = 56 * 1024 * 1024

LRU_TILE = 640
ATT_TILE = 640
TOK_TILE = 512
ROUTE_TILE = 256
EXPERT_CHUNK = 1024
GATE_BLOCK = 32
GATE_ROWS = 4
PAGES_PER_STEP = 8


def _cparams(sem):
    return pltpu.CompilerParams(dimension_semantics=sem, vmem_limit_bytes=VMEM_LIMIT)


def _rmsnorm(x, g):
    ms = jnp.mean(x * x, axis=-1, keepdims=True)
    return x * lax.rsqrt(ms + EPS) * g


def _full(shape):
    n = len(shape)
    return pl.BlockSpec(shape, lambda *_: (0,) * n)


def _lru_kernel(x_ref, h0_ref, buf0_ref, g_ref, wg_ref, wi_ref, cw_ref, cb_ref, wa_ref, ba_ref,
                wx_ref, bx_ref, lam_ref, wo_ref,
                xo_ref, ht_ref, buft_ref,
                ubuf, a_s, u_s, hs_s, h_s, *, nseq, rows, reset_first, last_t):
    ti = pl.program_id(1)
    d = x_ref.shape[-1]
    tail = (CONV_W - 1) * nseq
    off = -(-tail // SUBLANE) * SUBLANE

    @pl.when(ti == 0)
    def _():
        h_s[...] = h0_ref[0]
        ubuf[off - tail:off, :] = buf0_ref[0]

    x = x_ref[...]
    xb = _rmsnorm(x, g_ref[...]).astype(BF16)
    gate = jax.nn.gelu(jnp.dot(xb, wg_ref[...], preferred_element_type=F32))
    ubuf[off:off + rows, :] = jnp.dot(xb, wi_ref[...], preferred_element_type=F32)

    y = cb_ref[...]
    for k in range(CONV_W):
        s = off - (CONV_W - 1 - k) * nseq
        y = y + ubuf[s:s + rows, :] * cw_ref[k:k + 1, :]

    steps = rows // nseq
    lt_tile, lt_loc = divmod(last_t, steps)

    @pl.when(ti == lt_tile)
    def _():
        s = off + (lt_loc - (CONV_W - 2)) * nseq
        buft_ref[0] = ubuf[s:s + tail, :]

    ubuf[off - tail:off, :] = ubuf[off + rows - tail:off + rows, :]

    lam = -lam_ref[...]
    softplus = jnp.maximum(lam, 0.0) + jnp.log1p(jnp.exp(-jnp.abs(lam)))
    nblk = wa_ref.shape[0]
    bw = d // nblk
    if reset_first:
        row = lax.broadcasted_iota(jnp.int32, (rows, bw), 0)
        is_t0 = row < jnp.where(ti == 0, nseq, 0)
    for n in range(nblk):
        sl = slice(n * bw, (n + 1) * bw)
        yn = y[:, sl]
        ybn = yn.astype(BF16)
        r = jax.nn.sigmoid(jnp.dot(ybn, wa_ref[n], preferred_element_type=F32) + ba_ref[:, sl])
        i = jax.nn.sigmoid(jnp.dot(ybn, wx_ref[n], preferred_element_type=F32) + bx_ref[:, sl])
        log_a = -LRU_C * r * softplus[:, sl]
        a = jnp.exp(log_a)
        mult = jnp.sqrt(-jnp.tanh(log_a) * (a * a + 1.0))
        if reset_first:
            mult = jnp.where(is_t0, 1.0, mult)
        a_s[:, sl] = a
        u_s[:, sl] = mult * i * yn

    if nseq == 1:
        row8 = lax.broadcasted_iota(jnp.int32, (SUBLANE, d), 0)

        def group(gi, h):
            base = pl.multiple_of(gi * SUBLANE, SUBLANE)
            a8 = a_s[pl.ds(base, SUBLANE), :]
            u8 = u_s[pl.ds(base, SUBLANE), :]
            hs8 = jnp.zeros_like(a8)
            for r_ in range(SUBLANE):
                h = a8[r_:r_ + 1, :] * h + u8[r_:r_ + 1, :]
                hs8 = jnp.where(row8 == r_, h, hs8)
            hs_s[pl.ds(base, SUBLANE), :] = hs8
            return h
        h = lax.fori_loop(0, rows // SUBLANE, group, h_s[...])
    else:
        h = h_s[...]
        for t in range(steps):
            h = a_s[t * nseq:(t + 1) * nseq, :] * h + u_s[t * nseq:(t + 1) * nseq, :]
            hs_s[t * nseq:(t + 1) * nseq, :] = h
    h_s[...] = h

    @pl.when(ti == lt_tile)
    def _():
        ht_ref[0] = hs_s[lt_loc * nseq:(lt_loc + 1) * nseq, :]

    go = (gate * hs_s[...]).astype(BF16)
    xo_ref[...] = x + jnp.dot(go, wo_ref[...], preferred_element_type=F32)


def _lru_call(xall, h0, buf0, w, *, nbatch, nseq, rows, ntile, row0, reset_first, last_t):
    nt, d = xall.shape
    tail = (CONV_W - 1) * nseq
    off = -(-tail // SUBLANE) * SUBLANE
    blk0 = row0 // rows
    xmap = lambda b, t: (blk0 + b * ntile + t, 0)
    smap = lambda b, t: (b, 0, 0)
    kern = functools.partial(_lru_kernel, nseq=nseq, rows=rows, reset_first=reset_first,
                             last_t=last_t)
    return pl.pallas_call(
        kern,
        grid=(nbatch, ntile),
        in_specs=[pl.BlockSpec((rows, d), xmap),
                  pl.BlockSpec((1, nseq, d), smap),
                  pl.BlockSpec((1, tail, d), smap)] + [_full(a.shape) for a in w],
        out_specs=[pl.BlockSpec((rows, d), xmap),
                   pl.BlockSpec((1, nseq, d), smap),
                   pl.BlockSpec((1, tail, d), smap)],
        out_shape=[jax.ShapeDtypeStruct((nt, d), F32),
                   jax.ShapeDtypeStruct((nbatch, nseq, d), F32),
                   jax.ShapeDtypeStruct((nbatch, tail, d), F32)],
        scratch_shapes=[pltpu.VMEM((off + rows, d), F32),
                        pltpu.VMEM((rows, d), F32),
                        pltpu.VMEM((rows, d), F32),
                        pltpu.VMEM((rows, d), F32),
                        pltpu.VMEM((nseq, d), F32)],
        input_output_aliases={0: 0},
        compiler_params=_cparams(("arbitrary", "arbitrary")),
        name="rg_lru_block",
    )(xall, h0, buf0, *w)


def _top_values(v, n, masked=0):
    tops = []
    for _ in range(n):
        m = jnp.max(v, axis=0, keepdims=True)
        tops.append(m)
        v = jnp.where(v == m, -jnp.inf, v)
    removed = jnp.sum(jnp.where(v == -jnp.inf, 1.0, 0.0), axis=0, keepdims=True) - float(masked)
    return tops, removed


def _candidate_sums(a, b):
    k = len(a)
    assert k == 2 * SUBLANE
    t = a[0].shape[1]
    row16 = lax.broadcasted_iota(jnp.int32, (k, t), 0)
    row8 = lax.broadcasted_iota(jnp.int32, (SUBLANE, t), 0)
    b16 = jnp.zeros((k, t), F32)
    a_hi = jnp.zeros((SUBLANE, t), F32)
    for i in range(k):
        b16 = jnp.where(row16 == i, b[i], b16)
    for i in range(SUBLANE):
        a_hi = jnp.where(row8 == i, a[SUBLANE + i], a_hi)
    b8 = b16[:SUBLANE]
    parts = [a[0] + b16]
    masked = 0
    for p in range(1, SUBLANE):
        limit = k // (p + 1)
        parts.append(jnp.where(row8 < limit, a[p] + b8, -jnp.inf))
        masked += SUBLANE - limit
    parts.append(a_hi + b[0])
    return jnp.concatenate(parts, axis=0), masked


_NO_RANK = 1024.0


def _ranked_top(v, k):
    index = lax.broadcasted_iota(jnp.int32, v.shape, 0).astype(F32)
    rank = jnp.full(v.shape, _NO_RANK, F32)
    vals = []
    for t in range(k):
        m = jnp.max(v, axis=0, keepdims=True)
        first = jnp.min(jnp.where(v == m, index, float(v.shape[0])), axis=0, keepdims=True)
        hit = index == first
        rank = jnp.where(hit, float(t + 1), rank)
        vals.append(m)
        v = jnp.where(hit, -jnp.inf, v)
    return rank, vals


def _ranked_row_counts(a, b):
    k = len(a)
    t = a[0].shape[1]
    row = lax.broadcasted_iota(jnp.int32, (k, t), 0).astype(F32)
    b_col = jnp.zeros((k, t), F32)
    for q in range(k):
        b_col = jnp.where(row == float(q), b[q], b_col)
    cand = jnp.concatenate([a[p] + b_col for p in range(k)], axis=0)
    flat = lax.broadcasted_iota(jnp.int32, cand.shape, 0).astype(F32)
    taken = jnp.zeros((k, t), F32)
    z = jnp.zeros((1, t), F32)
    best = None
    for _ in range(k):
        m = jnp.max(cand, axis=0, keepdims=True)
        first = jnp.min(jnp.where(cand == m, flat, float(k * k)), axis=0, keepdims=True)
        taken = taken + jnp.where(row == jnp.floor(first / float(k)), 1.0, 0.0)
        best = m if best is None else best
        z = z + jnp.exp(m - best)
        cand = jnp.where(flat == first, -jnp.inf, cand)
    return taken, z


def _route_kernel(x_ref, g_ref, wqt_ref, keys_ref,
                  xnt_ref, th_ref, c1_ref, s2_ref, e2_ref, q_s):
    nheads = th_ref.shape[0]
    nkeys = th_ref.shape[1]
    xn = _rmsnorm(x_ref[...], g_ref[...])
    xnt = xn.T.astype(BF16)
    xnt_ref[...] = xnt
    q_s[...] = jnp.dot(wqt_ref[...], xnt, preferred_element_type=F32)

    def head(h, carry):
        tops, dup, expd, sc = [], [], [], []
        for c in range(2):
            r0 = pl.multiple_of((h * 2 + c) * nkeys, nkeys)
            qhc = q_s[pl.ds(r0, nkeys), :].astype(BF16)
            s = jnp.dot(keys_ref[h * 2 + c], qhc, preferred_element_type=F32)
            t, removed = _top_values(s, PEER_TOPK)
            tops.append(t)
            dup.append(removed != float(PEER_TOPK))
            expd.append(jnp.exp(s - t[0]))
            sc.append(s)
        cand, masked = _candidate_sums(tops[0][:PEER_TOPK], tops[1][:PEER_TOPK])
        best, removed = _top_values(cand, PEER_TOPK, masked)
        tau = best[PEER_TOPK - 1]
        z = jnp.ones_like(tau)
        for t in range(1, PEER_TOPK):
            z = z + jnp.exp(best[t] - best[0])
        tie = dup[0] | dup[1] | (removed != float(PEER_TOPK))
        row16 = lax.broadcasted_iota(jnp.int32, (PEER_TOPK, tau.shape[1]), 0)
        b16 = jnp.zeros((PEER_TOPK, tau.shape[1]), F32)
        for q in range(PEER_TOPK):
            b16 = jnp.where(row16 == q, tops[1][q], b16)
        th = jnp.full(sc[0].shape, jnp.inf, F32)
        for p in range(PEER_TOPK):
            a_p = tops[0][p]
            theta = jnp.min(jnp.where(a_p + b16 >= tau, b16, jnp.inf), axis=0, keepdims=True)
            th = jnp.where(sc[0] == a_p, theta, th)
        th_ref[h] = th
        c1_ref[h] = expd[0] / z
        s2_ref[h] = sc[1]
        e2_ref[h] = expd[1]

        @pl.when(jnp.max(jnp.where(tie, 1.0, 0.0)) > 0.0)
        def _():
            rank1, a = _ranked_top(sc[0], PEER_TOPK)
            rank2, b = _ranked_top(sc[1], PEER_TOPK)
            taken, z_ranked = _ranked_row_counts(a, b)
            width = jnp.zeros_like(sc[0])
            for p in range(PEER_TOPK):
                width = jnp.where(rank1 == float(p + 1), taken[p:p + 1, :], width)
            th_ref[h] = jnp.where(tie, -width, th)
            s2_ref[h] = jnp.where(tie, -rank2, sc[1])
            c1_ref[h] = expd[0] / jnp.where(tie, z_ranked, z)

        return carry

    lax.fori_loop(0, nheads, head, 0)


def _route_call(xall, g, wqt, keys):
    nt, d = xall.shape
    nhc, nkeys, _ = keys.shape
    nheads = nhc // 2
    tm = ROUTE_TILE
    tok = lambda i: (0, 0, i)
    big = jax.ShapeDtypeStruct((nheads, nkeys, nt), F32)
    return pl.pallas_call(
        _route_kernel,
        grid=(nt // tm,),
        in_specs=[pl.BlockSpec((tm, d), lambda i: (i, 0)), _full(g.shape), _full(wqt.shape),
                  _full(keys.shape)],
        out_specs=[pl.BlockSpec((d, tm), lambda i: (0, i))]
        + [pl.BlockSpec((nheads, nkeys, tm), tok)] * 4,
        out_shape=[jax.ShapeDtypeStruct((d, nt), BF16), big, big, big, big],
        scratch_shapes=[pltpu.VMEM((wqt.shape[0], tm), F32)],
        compiler_params=_cparams(("arbitrary",)),
        name="peer_route",
    )(xall, g, wqt, keys)


def _expert_kernel(x_ref, xnt_ref, th_ref, c1_ref, s2_ref, e2_ref, u_ref, vt_ref,
                   xo_ref, acc, a_s, w_s, *, jblk):
    c = pl.program_id(1)
    nheads, nkeys, tm = s2_ref.shape
    ec = u_ref.shape[0]
    nrow_i = ec // nkeys
    njb = nkeys // jblk
    slab = nrow_i * jblk
    mm_lanes = 2 * LANE
    i0 = pl.multiple_of(c * nrow_i, SUBLANE)

    @pl.when(c == 0)
    def _():
        acc[...] = jnp.zeros_like(acc)

    def gates(l0, jb):
        lanes = pl.ds(l0, LANE)
        jrows = slice(jb * jblk, (jb + 1) * jblk)
        for first in range(0, nrow_i, GATE_ROWS):
            iis = range(first, first + GATE_ROWS)
            g = {ii: jnp.zeros((jblk, LANE), F32) for ii in iis}
            for h in range(nheads):
                th = th_ref[h, pl.ds(i0, SUBLANE), lanes]
                c1 = c1_ref[h, pl.ds(i0, SUBLANE), lanes]
                s2 = s2_ref[h, jrows, lanes]
                e2 = e2_ref[h, jrows, lanes]
                for ii in iis:
                    picked = jnp.where(s2 >= th[ii:ii + 1, :], e2, 0.0)
                    g[ii] = g[ii] + picked * c1[ii:ii + 1, :]
            for ii in iis:
                rows = slice(jb * slab + ii * jblk, jb * slab + (ii + 1) * jblk)
                w_s[rows, lanes] = (g[ii] * jax.nn.gelu(a_s[rows, lanes])).astype(BF16)

    def piece(l, carry):
        m0 = pl.multiple_of(l * mm_lanes, mm_lanes)
        lanes = pl.ds(m0, mm_lanes)

        for jb in range(njb):
            rows = slice(jb * slab, (jb + 1) * slab)
            a_s[rows, lanes] = jnp.dot(u_ref[rows, :], xnt_ref[:, lanes],
                                       preferred_element_type=F32)
            for sub in range(mm_lanes // LANE):
                gates(pl.multiple_of(m0 + sub * LANE, LANE), jb)
            acc[:, lanes] += jnp.dot(vt_ref[:, rows], w_s[rows, lanes],
                                     preferred_element_type=F32)
        return carry

    lax.fori_loop(0, tm // mm_lanes, piece, 0)

    @pl.when(c == pl.num_programs(1) - 1)
    def _():
        xo_ref[...] = x_ref[...] + acc[...].T


def _expert_call(xall, xnt, th, c1, s2, e2, u, vt):
    nt, d = xall.shape
    nheads, nkeys, _ = th.shape
    nexp = u.shape[0]
    tm, ec = TOK_TILE, EXPERT_CHUNK
    tok = lambda i, c: (0, 0, i)
    return pl.pallas_call(
        functools.partial(_expert_kernel, jblk=GATE_BLOCK),
        grid=(nt // tm, nexp // ec),
        in_specs=[pl.BlockSpec((tm, d), lambda i, c: (i, 0)),
                  pl.BlockSpec((d, tm), lambda i, c: (0, i))]
        + [pl.BlockSpec((nheads, nkeys, tm), tok)] * 4
        + [pl.BlockSpec((ec, d), lambda i, c: (c, 0)),
           pl.BlockSpec((d, ec), lambda i, c: (0, c))],
        out_specs=pl.BlockSpec((tm, d), lambda i, c: (i, 0)),
        out_shape=jax.ShapeDtypeStruct((nt, d), F32),
        scratch_shapes=[pltpu.VMEM((d, tm), F32), pltpu.VMEM((ec, tm), F32),
                        pltpu.VMEM((ec, tm), BF16)],
        input_output_aliases={0: 0},
        compiler_params=_cparams(("arbitrary", "arbitrary")),
        name="peer_experts",
    )(xall, xnt, th, c1, s2, e2, u, vt)


def _slab_order(table, nkeys):
    nexp, d = table.shape
    nrow_i = EXPERT_CHUNK // nkeys
    t = table.reshape(nexp // EXPERT_CHUNK, nrow_i, nkeys // GATE_BLOCK, GATE_BLOCK, d)
    return t.transpose(0, 2, 1, 3, 4).reshape(nexp, d)


def _peer(xall, g, wqt, keys, u, vt):
    xnt, th, c1, s2, e2 = _route_call(xall, g, wqt, keys)
    return _expert_call(xall, xnt, th, c1, s2, e2, u, vt)


def _exact_group_sum(sq, ones_bd):
    hi = sq.astype(BF16)
    r1 = sq - hi.astype(F32)
    mid = r1.astype(BF16)
    lo = (r1 - mid.astype(F32)).astype(BF16)
    dot = lambda t: jnp.dot(t, ones_bd, preferred_element_type=F32)
    return dot(hi) + dot(mid) + dot(lo)


def _head_norm_rope(y, g2, cos, sin_signed, ones_bd, first_half, head_dim):
    out = []
    for n in range(y.shape[1] // LANE):
        blk = y[:, n * LANE:(n + 1) * LANE]
        ms = _exact_group_sum(blk * blk, ones_bd) / head_dim
        z = blk * lax.rsqrt(ms + EPS) * g2
        partner = jnp.where(first_half, pltpu.roll(z, LANE - head_dim // 2, 1),
                            pltpu.roll(z, head_dim // 2, 1))
        out.append(z * cos + partner * sin_signed)
    return jnp.concatenate(out, axis=1)


def _qkv_kernel(x_ref, cos_ref, sin_ref, gkv_ref, gq_ref, wk_ref, wv_ref, wq_ref, gk2_ref,
                gq2_ref, k_ref, v_ref, kb_ref, vt_ref, qt_ref, qb_ref, *, head_dim):
    x = x_ref[...]
    cos = cos_ref[...]
    sin_signed = sin_ref[...]
    lane = lax.broadcasted_iota(jnp.int32, (1, LANE), 1)
    first_half = (lane % head_dim) < head_dim // 2
    r = lax.broadcasted_iota(jnp.int32, (LANE, LANE), 0) // head_dim
    cidx = lax.broadcasted_iota(jnp.int32, (LANE, LANE), 1) // head_dim
    ones_bd = jnp.where(r == cidx, 1.0, 0.0).astype(BF16)

    hk = _rmsnorm(x, gkv_ref[...]).astype(BF16)
    k = _head_norm_rope(jnp.dot(hk, wk_ref[...], preferred_element_type=F32), gk2_ref[...],
                        cos, sin_signed, ones_bd, first_half, head_dim)
    v = jnp.dot(hk, wv_ref[...], preferred_element_type=F32)
    hq = _rmsnorm(x, gq_ref[...]).astype(BF16)
    q = _head_norm_rope(jnp.dot(hq, wq_ref[...], preferred_element_type=F32), gq2_ref[...],
                        cos, sin_signed, ones_bd, first_half, head_dim)
    k_ref[...] = k
    v_ref[...] = v
    kb_ref[...] = k.astype(BF16)
    vt_ref[...] = v.T.astype(BF16)
    qs = q * head_dim ** -0.5
    qt_ref[...] = qs.T.astype(BF16)
    qb_ref[...] = qs.astype(BF16)


def _qkv_call(xall, cos, sin_signed, gkv, gq, wk, wv, wq, gk2, gq2, head_dim):
    nt, d = xall.shape
    tm = TOK_TILE
    row = lambda i: (i, 0)
    ws = [gkv, gq, wk, wv, wq, gk2, gq2]
    return pl.pallas_call(
        functools.partial(_qkv_kernel, head_dim=head_dim),
        grid=(nt // tm,),
        in_specs=[pl.BlockSpec((tm, d), row), pl.BlockSpec((tm, LANE), row),
                  pl.BlockSpec((tm, LANE), row)] + [_full(a.shape) for a in ws],
        out_specs=[pl.BlockSpec((tm, d), row)] * 3 + [pl.BlockSpec((d, tm), lambda i: (0, i))] * 2
        + [pl.BlockSpec((tm, d), row)],
        out_shape=[jax.ShapeDtypeStruct((nt, d), F32)] * 2 + [jax.ShapeDtypeStruct((nt, d), BF16)]
        + [jax.ShapeDtypeStruct((d, nt), BF16)] * 2 + [jax.ShapeDtypeStruct((nt, d), BF16)],
        compiler_params=_cparams(("arbitrary",)),
        name="qkv_proj",
    )(xall, cos, sin_signed, *ws)


def _prompt_attn_kernel(*refs, head_dim, nstream):
    qt_refs = refs[:nstream]
    k_refs = refs[nstream:2 * nstream]
    vt_refs = refs[2 * nstream:3 * nstream]
    a0_ref, a1_ref = refs[3 * nstream:]
    qi = pl.program_id(1)
    tq = qt_refs[0].shape[1]
    tk = tq
    feat = lax.broadcasted_iota(jnp.int32, (LANE, 1), 0)
    qs = []
    for qt_ref in qt_refs:
        qt = qt_ref[...]
        zero = jnp.zeros_like(qt)
        qs.append(jnp.concatenate([jnp.where(feat < head_dim, qt, zero),
                                   jnp.where(feat >= head_dim, qt, zero)], axis=1))

    def update(carry, j, masked):
        k0 = pl.multiple_of(j * tk, tk)
        s = [jnp.dot(k_refs[b][pl.ds(k0, tk), :], qs[b], preferred_element_type=F32)
             for b in range(nstream)]
        out = []
        for b in range(nstream):
            m, l, acc = carry[b]
            sb = s[b]
            if masked:
                kpos = lax.broadcasted_iota(jnp.int32, (tk, 2 * tq), 0)
                qpos = lax.broadcasted_iota(jnp.int32, (tk, 2 * tq), 1) % tq
                sb = jnp.where(kpos <= qpos, sb, NEG)
            m_new = jnp.maximum(m, jnp.max(sb, axis=0, keepdims=True))
            corr = jnp.exp(m - m_new)
            p = jnp.exp(sb - m_new)
            l = l * corr + jnp.sum(p, axis=0, keepdims=True)
            acc = acc * corr + jnp.dot(vt_refs[b][:, pl.ds(k0, tk)], p.astype(BF16),
                                       preferred_element_type=F32)
            out.append((m_new, l, acc))
        return tuple(out)

    init = tuple((jnp.full((1, 2 * tq), NEG, F32), jnp.zeros((1, 2 * tq), F32),
                  jnp.zeros((LANE, 2 * tq), F32)) for _ in range(nstream))
    carry = lax.fori_loop(0, qi, lambda j, cr: update(cr, j, False), init)
    carry = update(carry, qi, True)
    for b, (m, l, acc) in enumerate(carry):
        out = acc / l
        a0_ref[b] = out[:, :tq]
        a1_ref[b] = out[:, tq:]


def _prompt_attn_call(qt, kb, vt, nbatch, tp, head_dim):
    d, nt = qt.shape
    nheads = d // LANE
    tq = ATT_TILE
    nq = tp // tq
    bind = lambda f, b: functools.partial(f, b=b)
    streams = range(nbatch)
    return pl.pallas_call(
        functools.partial(_prompt_attn_kernel, head_dim=head_dim, nstream=nbatch),
        grid=(nheads, nq),
        in_specs=[pl.BlockSpec((LANE, tq), bind(lambda h, i, b: (h, b * nq + i), b))
                  for b in streams]
        + [pl.BlockSpec((tp, LANE), bind(lambda h, i, b: (b, h), b)) for b in streams]
        + [pl.BlockSpec((LANE, tp), bind(lambda h, i, b: (h, b), b)) for b in streams],
        out_specs=[pl.BlockSpec((nbatch, LANE, tq), lambda h, i: (0, h, i))] * 2,
        out_shape=[jax.ShapeDtypeStruct((nbatch, d, tp), F32)] * 2,
        compiler_params=_cparams(("arbitrary", "arbitrary")),
        name="prompt_attention",
    )(*([qt] * nbatch), *([kb] * nbatch), *([vt] * nbatch))


def _decode_attn_kernel(pt_ref, q_ref, kn_ref, vn_ref, *refs, head_dim, npg):
    del pt_ref
    kp = refs[:npg]
    vp = refs[npg:2 * npg]
    o_ref, qblk, m_s, l_s, acc = refs[2 * npg:]
    p = pl.program_id(1)
    nrow, d = qblk.shape
    nheads = d // LANE
    per_head = nrow // nheads
    ntok = per_head // 2
    psz = kn_ref.shape[2]

    @pl.when(p == 0)
    def _():
        q = q_ref[0].astype(F32)
        row = lax.broadcasted_iota(jnp.int32, (nrow, d), 0)
        col = lax.broadcasted_iota(jnp.int32, (nrow, d), 1)
        rep = jnp.zeros((nrow, d), F32)
        for t in range(ntok):
            rep = jnp.where(row % ntok == t, q[t:t + 1, :], rep)
        mine = (col // LANE == row // per_head) & ((col % LANE) // head_dim == (row // ntok) % 2)
        qblk[...] = jnp.where(mine, rep, 0.0).astype(BF16)
        m_s[...] = jnp.full_like(m_s, NEG)
        l_s[...] = jnp.zeros_like(l_s)
        acc[...] = jnp.zeros_like(acc)

    def update(s, v_refs):
        m = m_s[...]
        m_new = jnp.maximum(m, jnp.max(s, axis=-1, keepdims=True))
        corr = jnp.exp(m - m_new)
        pr = jnp.exp(s - m_new)
        l_s[...] = l_s[...] * corr + jnp.sum(pr, axis=-1, keepdims=True)
        m_s[...] = m_new
        pr = pr.astype(BF16)
        for h in range(nheads):
            rows = slice(h * per_head, (h + 1) * per_head)
            pv = acc[rows, :] * corr[rows, :]
            for i, v_ref in enumerate(v_refs):
                vh = v_ref[0, pl.ds(h, psz, stride=nheads), :].astype(BF16)
                pv = pv + jnp.dot(pr[rows, i * psz:(i + 1) * psz], vh,
                                  preferred_element_type=F32)
            acc[rows, :] = pv

    qb = qblk[...]
    s = jnp.concatenate([jnp.dot(qb, kp[i][0].astype(BF16), preferred_element_type=F32)
                         for i in range(npg)], axis=1)
    update(s, vp)

    @pl.when(p == pl.num_programs(1) - 1)
    def _():
        s = jnp.dot(qblk[...], kn_ref[0].astype(BF16), preferred_element_type=F32)
        trow = lax.broadcasted_iota(jnp.int32, s.shape, 0) % ntok
        tcol = lax.broadcasted_iota(jnp.int32, s.shape, 1)
        update(jnp.where(tcol <= trow, s, NEG), [vn_ref])
        o_ref[0] = acc[...] / l_s[...]


def _decode_attn_call(page_table, q, knt, vn, cache_kt, cache_v, head_dim):
    nb, ntok, d = q.shape
    npages = page_table.shape[1]
    npg = PAGES_PER_STEP
    psz = cache_kt.shape[2]
    nheads = d // LANE
    nrow = 2 * ntok * nheads
    bmap = lambda b, p, pt: (b, 0, 0)
    pmaps = [functools.partial(lambda b, p, pt, i: (pt[b, p * npg + i], 0, 0), i=i)
             for i in range(npg)]
    grid_spec = pltpu.PrefetchScalarGridSpec(
        num_scalar_prefetch=1,
        grid=(nb, npages // npg),
        in_specs=[pl.BlockSpec((1, ntok, d), bmap), pl.BlockSpec((1, d, psz), bmap),
                  pl.BlockSpec((1, psz * nheads, LANE), bmap)]
        + [pl.BlockSpec((1, d, psz), pm) for pm in pmaps]
        + [pl.BlockSpec((1, psz * nheads, LANE), pm) for pm in pmaps],
        out_specs=pl.BlockSpec((1, nrow, LANE), bmap),
        scratch_shapes=[pltpu.VMEM((nrow, d), BF16),
                        pltpu.VMEM((nrow, 1), F32),
                        pltpu.VMEM((nrow, 1), F32),
                        pltpu.VMEM((nrow, LANE), F32)])
    return pl.pallas_call(
        functools.partial(_decode_attn_kernel, head_dim=head_dim, npg=npg),
        grid_spec=grid_spec,
        out_shape=jax.ShapeDtypeStruct((nb, nrow, LANE), F32),
        compiler_params=_cparams(("arbitrary", "arbitrary")),
        name="decode_attention",
    )(page_table, q, knt, vn, *([cache_kt] * npg), *([cache_v] * npg))


def _combine_kernel(x_ref, a0_ref, a1_ref, lp_ref, gs_ref, wot_ref, xo_ref, *, lambda_init):
    lp = lp_ref[...]
    lam = (jnp.exp(jnp.sum(lp[0:1] * lp[1:2], axis=-1, keepdims=True))
           - jnp.exp(jnp.sum(lp[2:3] * lp[3:4], axis=-1, keepdims=True)) + lambda_init)
    o = a0_ref[0] - lam * a1_ref[0]
    parts = []
    for h in range(o.shape[0] // LANE):
        oh = o[h * LANE:(h + 1) * LANE, :]
        ms = jnp.mean(oh * oh, axis=0, keepdims=True)
        parts.append(oh * lax.rsqrt(ms + EPS) * gs_ref[...] * (1.0 - lambda_init))
    ob = jnp.concatenate(parts, axis=0).astype(BF16)
    yt = jnp.dot(wot_ref[...], ob, preferred_element_type=F32)
    xo_ref[...] = x_ref[...] + yt.T


def _combine_call(xall, a0, a1, lp, gs, wot, lambda_init, *, row0, tile):
    nt, d = xall.shape
    ngroup, _, t = a0.shape
    per_group = t // tile
    row = lambda i: (row0 // tile + i, 0)
    col = lambda i: (i // per_group, 0, i % per_group)
    return pl.pallas_call(
        functools.partial(_combine_kernel, lambda_init=lambda_init),
        grid=(ngroup * per_group,),
        in_specs=[pl.BlockSpec((tile, d), row), pl.BlockSpec((1, d, tile), col),
                  pl.BlockSpec((1, d, tile), col), _full(lp.shape), _full(gs.shape),
                  _full(wot.shape)],
        out_specs=pl.BlockSpec((tile, d), row),
        out_shape=jax.ShapeDtypeStruct((nt, d), F32),
        input_output_aliases={0: 0},
        compiler_params=_cparams(("arbitrary",)),
        name="diff_combine_out_proj",
    )(xall, a0, a1, lp, gs, wot)


def _round_up(n, m):
    return -(-n // m) * m


def kernel(x_prompt, x_sample, state_lru_h, state_conv, cache_k, cache_v, page_table, meta_tokens, g_mix, g_ffn, lru_w_gate, lru_w_in, lru_conv_w, lru_conv_b, lru_w_a, lru_b_a, lru_w_x, lru_b_x, lru_lambda, lru_w_out, kv_norm, w_k, w_v, k_norm, w_q, q_norm, lam_params, sub_norm, w_o, peer_w_q, peer_keys, peer_u, peer_v):
    nbatch, seq, d = x_prompt.shape
    nb_dec, dec_seq, _ = x_sample.shape
    depth = g_mix.shape[0]
    assert depth == 2 and lru_w_gate.shape[0] == 1 and w_q.shape[0] == 1
    head_dim = k_norm.shape[0]
    nheads = d // (2 * head_dim)
    assert 2 * head_dim == LANE
    past_len = page_table.shape[1] * cache_k.shape[1]
    t_real = seq + N_META
    tp = _round_up(t_real, math.lcm(LRU_TILE, ATT_TILE))
    ns = nb_dec * dec_seq
    row_s = nbatch * tp
    assert row_s % ns == 0
    nt = _round_up(row_s + ns, math.lcm(TOK_TILE, ROUTE_TILE))
    row2 = lambda a: a.reshape(1, -1)

    xall = jnp.zeros((nt, d), F32)
    for b in range(nbatch):
        xall = lax.dynamic_update_slice(xall, meta_tokens.astype(F32), (b * tp, 0))
        xall = lax.dynamic_update_slice(xall, x_prompt[b], (b * tp + N_META, 0))
    xall = lax.dynamic_update_slice(xall, x_sample.transpose(1, 0, 2).reshape(ns, d), (row_s, 0))

    lw = [row2(g_mix[0]), lru_w_gate[0].astype(BF16), lru_w_in[0].astype(BF16), lru_conv_w[0],
          row2(lru_conv_b[0]), lru_w_a[0].astype(BF16), row2(lru_b_a[0]),
          lru_w_x[0].astype(BF16), row2(lru_b_x[0]), row2(lru_lambda[0]),
          lru_w_out[0].astype(BF16)]
    xall, h_p, buf_p = _lru_call(
        xall, jnp.zeros((nbatch, 1, d), F32), jnp.zeros((nbatch, CONV_W - 1, d), F32), lw,
        nbatch=nbatch, nseq=1, rows=LRU_TILE, ntile=tp // LRU_TILE, row0=0, reset_first=True,
        last_t=t_real - 1)
    buf0_s = state_conv[0].transpose(1, 0, 2).reshape(1, (CONV_W - 1) * nb_dec, d)
    xall, h_s, buf_s = _lru_call(
        xall, state_lru_h[0][None], buf0_s, lw,
        nbatch=1, nseq=nb_dec, rows=ns, ntile=1, row0=row_s, reset_first=False,
        last_t=dec_seq - 1)

    def peer_layer(x, layer):
        keys = peer_keys[layer].reshape(-1, peer_keys.shape[3], peer_keys.shape[4]).astype(BF16)
        nkeys = peer_keys.shape[3]
        return _peer(x, row2(g_ffn[layer]), peer_w_q[layer].T.astype(BF16), keys,
                     _slab_order(peer_u[layer].astype(BF16), nkeys),
                     _slab_order(peer_v[layer].astype(BF16), nkeys).T)

    xall = peer_layer(xall, 0)

    half = head_dim // 2
    inv = ROPE_THETA ** (-jnp.arange(half, dtype=F32) / half)
    rows = jnp.arange(nt)
    pos = jnp.where(rows < row_s, rows % tp,
                    jnp.where(rows < row_s + ns, past_len + (rows - row_s) // nb_dec, 0))
    ang = pos.astype(F32)[:, None] * inv[None, :]
    cos = jnp.tile(jnp.cos(ang), (1, LANE // half))
    sin = jnp.sin(ang)
    sin_signed = jnp.tile(jnp.concatenate([-sin, sin], axis=1), (1, LANE // head_dim))
    tile2 = lambda g: jnp.tile(g, LANE // head_dim).reshape(1, LANE)
    k_all, v_all, kb, vt, qt, qb = _qkv_call(
        xall, cos, sin_signed, row2(kv_norm), row2(g_mix[1]), w_k.astype(BF16),
        w_v.astype(BF16), w_q[0].astype(BF16), tile2(k_norm), tile2(q_norm[0]), head_dim)

    a0p, a1p = _prompt_attn_call(qt, kb, vt, nbatch, tp, head_dim)

    to_seq = lambda a: a[row_s:row_s + ns].reshape(dec_seq, nb_dec, d).transpose(1, 0, 2)
    npool, psz = cache_k.shape[:2]
    pad_tok = ((0, 0), (0, psz - dec_seq), (0, 0))
    knt = jnp.pad(to_seq(k_all), pad_tok).transpose(0, 2, 1)
    vn = jnp.pad(to_seq(v_all), pad_tok).reshape(nb_dec, psz * nheads, LANE)
    o_s = _decode_attn_call(
        page_table, to_seq(qb), knt, vn,
        cache_k.reshape(npool, psz, d).transpose(0, 2, 1),
        cache_v.reshape(npool, psz * nheads, LANE), head_dim)
    o_s = o_s.reshape(nb_dec, nheads, 2, dec_seq, LANE).transpose(0, 2, 3, 1, 4)
    o_s = o_s.reshape(nb_dec, 2, dec_seq, d)
    from_seq_t = lambda a: a.transpose(2, 1, 0).reshape(1, d, ns)

    lambda_init = 0.8 - 0.6 * math.exp(-0.3 * 1)
    combine = functools.partial(_combine_call, lp=lam_params[0], gs=sub_norm[0].reshape(-1, 1),
                                wot=w_o[0].T.astype(BF16), lambda_init=lambda_init)
    xall = combine(xall, a0p, a1p, row0=0, tile=ATT_TILE)
    xall = combine(xall, from_seq_t(o_s[:, 0]), from_seq_t(o_s[:, 1]), row0=row_s, tile=ns)
    xall = peer_layer(xall, 1)

    prompt = lambda a, t0: jnp.stack([a[b * tp + t0:b * tp + t_real] for b in range(nbatch)])
    y_prompt = prompt(xall, N_META)
    y_sample = to_seq(xall)
    k_p = prompt(k_all, 0).reshape(nbatch, t_real, nheads, 2, head_dim)
    v_p = prompt(v_all, 0).reshape(nbatch, t_real, nheads, 2 * head_dim)
    k_s = to_seq(k_all).reshape(nb_dec, dec_seq, nheads, 2, head_dim)
    v_s = to_seq(v_all).reshape(nb_dec, dec_seq, nheads, 2 * head_dim)
    conv_s = buf_s.reshape(CONV_W - 1, nb_dec, d).transpose(1, 0, 2)[None]
    return (y_prompt, y_sample, h_p.reshape(1, nbatch, d), buf_p[None], k_p, v_p,
            h_s, conv_s, k_s, v_s)
```
